```python
import math
import jax, jax.numpy as jnp
from jax import lax
import numpy as np

D_MODEL = 4096
BATCH = 1
SEQ = 16384
DEPTH = 4

N_MEM = 256
HEAD_DIM = 128
MEM_HEADS = 4
MEM_WIDTH = D_MODEL // 4
MEM_HEAD_DIM = MEM_WIDTH // MEM_HEADS
MIX_WIDTH = D_MODEL - MEM_WIDTH
DIFF_HEADS = MIX_WIDTH // (2 * HEAD_DIM)
FOX_HEADS = MIX_WIDTH // HEAD_DIM
ROT_DIM = HEAD_DIM // 4
ROPE_THETA = 500000.0
BLOCK_Q = 128
NORM_EPS = 1e-6
N_DIFF = (DEPTH + 1) // 2
N_FOX = DEPTH // 2
DIFF_IN = 4 * MIX_WIDTH + 2 * MEM_WIDTH
FOX_IN = 4 * MIX_WIDTH + FOX_HEADS + 2 * MEM_WIDTH

kernel_name = "hybrid_diff_fox_memory_trunk"


def rmsnorm(t, g):
    tf = t.astype(jnp.float32)
    y = tf * lax.rsqrt(jnp.mean(tf * tf, axis=-1, keepdims=True) + NORM_EPS)
    return (y * g.astype(jnp.float32)).astype(t.dtype)


def rope_tables(seq):
    inv = jnp.power(ROPE_THETA, -jnp.arange(0, ROT_DIM, 2, dtype=jnp.float32) / ROT_DIM)
    ang = jnp.arange(seq, dtype=jnp.float32)[:, None] * inv[None, :]
    return jnp.cos(ang), jnp.sin(ang)


def apply_partial_rope(t, cos, sin):
    s = t.shape[1]
    half = ROT_DIM // 2
    shp = (1, s) + (1,) * (t.ndim - 3) + (half,)
    c = cos.reshape(shp).astype(t.dtype)
    sn = sin.reshape(shp).astype(t.dtype)
    t1, t2, rest = t[..., :half], t[..., half:ROT_DIM], t[..., ROT_DIM:]
    return jnp.concatenate([t1 * c - t2 * sn, t2 * c + t1 * sn, rest], axis=-1)


def diff_attend(q, k, v, lam):
    b, s = q.shape[0], q.shape[1]
    nb = s // BLOCK_Q
    scale = HEAD_DIM ** -0.5
    kpos = jnp.arange(s)

    def one(i):
        st = i * BLOCK_Q
        qb = lax.dynamic_slice_in_dim(q, st, BLOCK_Q, axis=1)
        sc = jnp.einsum('bqchd,bkchd->bchqk', qb, k).astype(jnp.float32) * scale
        mask = (st + jnp.arange(BLOCK_Q))[:, None] >= kpos[None, :]
        p = jax.nn.softmax(jnp.where(mask, sc, -jnp.inf), axis=-1)
        a = p[:, 0] - lam * p[:, 1]
        return jnp.einsum('bhqk,bkhe->bqhe', a.astype(v.dtype), v)

    o = lax.map(one, jnp.arange(nb))
    return o.transpose(1, 0, 2, 3, 4).reshape(b, s, v.shape[2], v.shape[3])


def fox_attend(q, k, v, logf):
    b, s, h, d = q.shape
    nb = s // BLOCK_Q
    scale = HEAD_DIM ** -0.5
    c = jnp.cumsum(logf, axis=1).transpose(0, 2, 1)
    kpos = jnp.arange(s)

    def one(i):
        st = i * BLOCK_Q
        qb = lax.dynamic_slice_in_dim(q, st, BLOCK_Q, axis=1)
        cq = lax.dynamic_slice_in_dim(c, st, BLOCK_Q, axis=2)
        sc = jnp.einsum('bqhd,bkhd->bhqk', qb, k).astype(jnp.float32) * scale
        sc = sc + (cq[:, :, :, None] - c[:, :, None, :])
        mask = (st + jnp.arange(BLOCK_Q))[:, None] >= kpos[None, :]
        p = jax.nn.softmax(jnp.where(mask, sc, -jnp.inf), axis=-1)
        return jnp.einsum('bhqk,bkhd->bqhd', p.astype(v.dtype), v)

    o = lax.map(one, jnp.arange(nb))
    return o.transpose(1, 0, 2, 3, 4).reshape(b, s, h, d)


def diff_branch(h, w_in, q_g, k_g, lq1, lk1, lq2, lk2, sub_g, lam_init, cos, sin):
    b, s, _ = h.shape
    p = h @ w_in
    q, k, v, g, mq, mg = jnp.split(
        p, [MIX_WIDTH, 2 * MIX_WIDTH, 3 * MIX_WIDTH, 4 * MIX_WIDTH, 4 * MIX_WIDTH + MEM_WIDTH], axis=-1)
    q = apply_partial_rope(rmsnorm(q.reshape(b, s, 2, DIFF_HEADS, HEAD_DIM), q_g), cos, sin)
    k = apply_partial_rope(rmsnorm(k.reshape(b, s, 2, DIFF_HEADS, HEAD_DIM), k_g), cos, sin)
    v = v.reshape(b, s, DIFF_HEADS, 2 * HEAD_DIM)
    f32 = jnp.float32
    lam = (jnp.exp(jnp.sum(lq1.astype(f32) * lk1.astype(f32)))
           - jnp.exp(jnp.sum(lq2.astype(f32) * lk2.astype(f32))) + lam_init)
    o = diff_attend(q, k, v, lam)
    o = rmsnorm(o, sub_g) * (1.0 - lam_init)
    o = o.reshape(b, s, MIX_WIDTH) * jax.nn.silu(g)
    return o, mq, mg


def fox_branch(h, w_in, b_f, q_g, k_g):
    b, s, _ = h.shape
    p = h @ w_in
    q, k, v, g, fl, mq, mg = jnp.split(
        p, [MIX_WIDTH, 2 * MIX_WIDTH, 3 * MIX_WIDTH, 4 * MIX_WIDTH, 4 * MIX_WIDTH + FOX_HEADS,
            4 * MIX_WIDTH + FOX_HEADS + MEM_WIDTH], axis=-1)
    q = rmsnorm(q.reshape(b, s, FOX_HEADS, HEAD_DIM), q_g)
    k = rmsnorm(k.reshape(b, s, FOX_HEADS, HEAD_DIM), k_g)
    v = v.reshape(b, s, FOX_HEADS, HEAD_DIM)
    logf = jax.nn.log_sigmoid(fl.astype(jnp.float32) + b_f.astype(jnp.float32))
    o = fox_attend(q, k, v, logf).reshape(b, s, MIX_WIDTH)
    return o * jax.nn.silu(g), mq, mg


def mem_branch(mq, mg, kv, q_g, k_g):
    b, s, _ = mq.shape
    q = rmsnorm(mq.reshape(b, s, MEM_HEADS, MEM_HEAD_DIM), q_g)
    km, vm = jnp.split(kv, 2, axis=-1)
    km = rmsnorm(km.reshape(b, N_MEM, MEM_HEADS, MEM_HEAD_DIM), k_g)
    vm = vm.reshape(b, N_MEM, MEM_HEADS, MEM_HEAD_DIM)
    sc = jnp.einsum('bshd,bmhd->bhsm', q, km).astype(jnp.float32) * (MEM_HEAD_DIM ** -0.5)
    p = jax.nn.softmax(sc, axis=-1)
    o = jnp.einsum('bhsm,bmhd->bshd', p.astype(vm.dtype), vm).reshape(b, s, MEM_WIDTH)
    return o * jax.nn.silu(mg)


def setup_inputs(seed: int = 0) -> dict:
    key = jax.random.key(seed)
    ks = jax.random.split(key, 24)
    f32 = jnp.float32
    nrm = lambda k, shp, sc: jax.random.normal(k, shp, f32) * sc
    gain = lambda k, shp: 1.0 + 0.02 * jax.random.normal(k, shp, f32)
    ws = D_MODEL ** -0.5
    return {
        "x": nrm(ks[0], (BATCH, SEQ, D_MODEL), 1.0),
        "mem": nrm(ks[1], (BATCH, N_MEM, D_MODEL), 1.0),
        "mem_norm_g": gain(ks[2], (D_MODEL,)),
        "ln_g": gain(ks[3], (DEPTH, D_MODEL)),
        "w_out": nrm(ks[4], (DEPTH, D_MODEL, D_MODEL), ws),
        "w_mem_kv": nrm(ks[5], (DEPTH, D_MODEL, 2 * MEM_WIDTH), ws),
        "mem_q_norm_g": gain(ks[6], (DEPTH, MEM_HEAD_DIM)),
        "mem_k_norm_g": gain(ks[7], (DEPTH, MEM_HEAD_DIM)),
        "diff_w_in": nrm(ks[8], (N_DIFF, D_MODEL, DIFF_IN), ws),
        "diff_q_norm_g": gain(ks[9], (N_DIFF, HEAD_DIM)),
        "diff_k_norm_g": gain(ks[10], (N_DIFF, HEAD_DIM)),
        "diff_lam_q1": nrm(ks[11], (N_DIFF, HEAD_DIM), 0.1),
        "diff_lam_k1": nrm(ks[12], (N_DIFF, HEAD_DIM), 0.1),
        "diff_lam_q2": nrm(ks[13], (N_DIFF, HEAD_DIM), 0.1),
        "diff_lam_k2": nrm(ks[14], (N_DIFF, HEAD_DIM), 0.1),
        "diff_subln_g": gain(ks[15], (N_DIFF, 2 * HEAD_DIM)),
        "fox_w_in": nrm(ks[16], (N_FOX, D_MODEL, FOX_IN), ws),
        "fox_b_f": jax.random.uniform(ks[17], (N_FOX, FOX_HEADS), f32, 1.0, 5.0),
        "fox_q_norm_g": gain(ks[18], (N_FOX, HEAD_DIM)),
        "fox_k_norm_g": gain(ks[19], (N_FOX, HEAD_DIM)),
    }


def reference(x, mem, mem_norm_g, ln_g, w_out, w_mem_kv, mem_q_norm_g, mem_k_norm_g,
              diff_w_in, diff_q_norm_g, diff_k_norm_g, diff_lam_q1, diff_lam_k1,
              diff_lam_q2, diff_lam_k2, diff_subln_g,
              fox_w_in, fox_b_f, fox_q_norm_g, fox_k_norm_g):
    cos, sin = rope_tables(x.shape[1])
    mem_n = rmsnorm(mem, mem_norm_g)
    for i in range(DEPTH):
        h = rmsnorm(x, ln_g[i])
        j = i // 2
        if i % 2 == 0:
            lam_init = 0.8 - 0.6 * math.exp(-0.3 * i)
            mix, mq, mg = diff_branch(h, diff_w_in[j], diff_q_norm_g[j], diff_k_norm_g[j],
                                      diff_lam_q1[j], diff_lam_k1[j], diff_lam_q2[j],
                                      diff_lam_k2[j], diff_subln_g[j], lam_init, cos, sin)
        else:
            mix, mq, mg = fox_branch(h, fox_w_in[j], fox_b_f[j], fox_q_norm_g[j], fox_k_norm_g[j])
        kv = mem_n @ w_mem_kv[i]
        memo = mem_branch(mq, mg, kv, mem_q_norm_g[i], mem_k_norm_g[i])
        x = x + jnp.concatenate([mix, memo], axis=-1) @ w_out[i]
    return x
```

```python
import functools
import math

import jax
import jax.numpy as jnp
from jax import lax
from jax.experimental import pallas as pl
from jax.experimental.pallas import tpu as pltpu

HEAD_DIM = 128
MEM_HEADS = 4
ROT_DIM = HEAD_DIM // 4
ROPE_THETA = 500000.0
NORM_EPS = 1e-6
LOG2E = 1.4426950408889634
NEG_BIG = -1e30

LANES = 128
V7X_VMEM_BYTES = 64 * 1024 * 1024
VMEM_LIMIT = V7X_VMEM_BYTES - 8 * 1024 * 1024
ATTN_TILE = 512

F32 = jnp.float32
BF16 = jnp.bfloat16
_NT = (((1,), (1,)), ((), ()))


def _params(n_axes):
    return pltpu.CompilerParams(
        dimension_semantics=("arbitrary",) * n_axes, vmem_limit_bytes=VMEM_LIMIT)


def _tile_rows(n, target):
    t = min(n, target)
    assert n % t == 0
    return t


def _lane_tile(x, reps):
    return x if reps == 1 else jnp.concatenate([x] * reps, axis=1)


def _silu(x):
    return x / (1.0 + jnp.exp(-x))


def _rms_cast_kernel(x_ref, g_ref, o_ref):
    x = x_ref[...]
    y = x * lax.rsqrt(jnp.mean(x * x, axis=-1, keepdims=True) + NORM_EPS)
    o_ref[...] = (y * g_ref[...]).astype(BF16)


def _rms_cast(x, g):
    s, d = x.shape
    tm = _tile_rows(s, 256)
    return pl.pallas_call(
        _rms_cast_kernel,
        grid=(s // tm,),
        in_specs=[pl.BlockSpec((tm, d), lambda i: (i, 0)),
                  pl.BlockSpec((1, d), lambda i: (0, 0))],
        out_specs=pl.BlockSpec((tm, d), lambda i: (i, 0)),
        out_shape=jax.ShapeDtypeStruct((s, d), BF16),
        compiler_params=_params(1),
        name="rms_cast",
    )(x, g.reshape(1, d))


def _in_proj_kernel(h_ref, w_ref, gq_ref, gk_ref, gm_ref, cos_ref, sa_ref, sb_ref, o_ref, acc_ref,
                    *, rope, mix_tiles, q_scale, mq_scale, mem_hd):
    n = pl.program_id(1)
    tn = acc_ref.shape[1]
    acc_ref[...] = jnp.dot(h_ref[...], w_ref[...], preferred_element_type=F32)

    def qk_epilogue(gain):
        for hh in range(tn // HEAD_DIM):
            sl = slice(hh * HEAD_DIM, (hh + 1) * HEAD_DIM)
            t = acc_ref[:, sl]
            t = t * lax.rsqrt(jnp.mean(t * t, axis=-1, keepdims=True) + NORM_EPS) * gain
            if rope:
                t = (t * cos_ref[...] + pltpu.roll(t, HEAD_DIM - ROT_DIM // 2, 1) * sa_ref[...]
                     + pltpu.roll(t, ROT_DIM // 2, 1) * sb_ref[...])
            o_ref[:, sl] = t.astype(BF16)

    @pl.when(n < mix_tiles)
    def _():
        qk_epilogue(gq_ref[...] * q_scale)

    @pl.when(jnp.logical_and(n >= mix_tiles, n < 2 * mix_tiles))
    def _():
        qk_epilogue(gk_ref[...])

    @pl.when(jnp.logical_and(n >= 2 * mix_tiles, n < 3 * mix_tiles))
    def _():
        o_ref[...] = acc_ref[...].astype(BF16)

    @pl.when(jnp.logical_or(jnp.logical_and(n >= 3 * mix_tiles, n < 4 * mix_tiles), n == 4 * mix_tiles + 1))
    def _():
        o_ref[...] = _silu(acc_ref[...]).astype(BF16)

    @pl.when(n == 4 * mix_tiles)
    def _():
        gain = gm_ref[...] * mq_scale
        for hh in range(tn // mem_hd):
            sl = slice(hh * mem_hd, (hh + 1) * mem_hd)
            t = acc_ref[:, sl]
            t = t * lax.rsqrt(jnp.mean(t * t, axis=-1, keepdims=True) + NORM_EPS) * gain
            o_ref[:, sl] = t.astype(BF16)


def _in_proj(h, w, gq, gk, gm, rope_tabs, *, rope, mem_width):
    s, d = h.shape
    n_out = w.shape[1]
    tn = mem_width
    mem_hd = mem_width // MEM_HEADS
    mix_tiles = (d - mem_width) // tn
    assert n_out == (4 * mix_tiles + 2) * tn
    tm = _tile_rows(s, 512)
    cos_t, sa_t, sb_t = rope_tabs
    kern = functools.partial(
        _in_proj_kernel, rope=rope, mix_tiles=mix_tiles,
        q_scale=HEAD_DIM ** -0.5 * LOG2E, mq_scale=mem_hd ** -0.5 * LOG2E, mem_hd=mem_hd)
    row_tab = pl.BlockSpec((tm, HEAD_DIM), lambda i, n: (i, 0))
    return pl.pallas_call(
        kern,
        grid=(s // tm, n_out // tn),
        in_specs=[pl.BlockSpec((tm, d), lambda i, n: (i, 0)),
                  pl.BlockSpec((d, tn), lambda i, n: (0, n)),
                  pl.BlockSpec((1, HEAD_DIM), lambda i, n: (0, 0)),
                  pl.BlockSpec((1, HEAD_DIM), lambda i, n: (0, 0)),
                  pl.BlockSpec((1, mem_hd), lambda i, n: (0, 0)),
                  row_tab, row_tab, row_tab],
        out_specs=pl.BlockSpec((tm, tn), lambda i, n: (i, n)),
        out_shape=jax.ShapeDtypeStruct((s, n_out), BF16),
        scratch_shapes=[pltpu.VMEM((tm, tn), F32)],
        compiler_params=_params(2),
        name="in_proj",
    )(h, w, gq.reshape(1, -1), gk.reshape(1, -1), gm.reshape(1, -1), cos_t, sa_t, sb_t)


def _fox_gate_kernel(h_ref, w_ref, b_ref, o_ref, carry_ref):
    @pl.when(pl.program_id(0) == 0)
    def _():
        carry_ref[...] = jnp.zeros_like(carry_ref)

    tm = h_ref.shape[0]
    fl = jnp.dot(h_ref[...], w_ref[...], preferred_element_type=F32) + b_ref[...]
    ls = jnp.minimum(fl, 0.0) - jnp.log1p(jnp.exp(-jnp.abs(fl)))
    hi = ls.astype(BF16)
    r1 = ls - hi.astype(F32)
    mid = r1.astype(BF16)
    lo = (r1 - mid.astype(F32)).astype(BF16)
    row = lax.broadcasted_iota(jnp.int32, (tm, tm), 0)
    col = lax.broadcasted_iota(jnp.int32, (tm, tm), 1)
    tri = jnp.where(row >= col, 1.0, 0.0).astype(BF16)
    c = (jnp.dot(tri, hi, preferred_element_type=F32)
         + jnp.dot(tri, mid, preferred_element_type=F32)
         + jnp.dot(tri, lo, preferred_element_type=F32)) + carry_ref[...]
    carry_ref[...] = c[tm - 1:tm, :]
    o_ref[...] = c * (-LOG2E)


def _fox_gates(h, w_fl, b_f):
    s, d = h.shape
    tm = _tile_rows(s, 512)
    return pl.pallas_call(
        _fox_gate_kernel,
        grid=(s // tm,),
        in_specs=[pl.BlockSpec((tm, d), lambda i: (i, 0)),
                  pl.BlockSpec((d, LANES), lambda i: (0, 0)),
                  pl.BlockSpec((1, LANES), lambda i: (0, 0))],
        out_specs=pl.BlockSpec((tm, LANES), lambda i: (i, 0)),
        out_shape=jax.ShapeDtypeStruct((s, LANES), F32),
        scratch_shapes=[pltpu.VMEM((1, LANES), F32)],
        compiler_params=_params(1),
        name="fox_gates",
    )(h, w_fl, b_f)


def _softmax_step(s, vb, m_ref, l_ref, acc_ref, c):
    tk = s.shape[1]
    dv = vb.shape[1]
    m_prev = m_ref[c]
    m_new = jnp.maximum(m_prev, jnp.max(s, axis=1, keepdims=True))
    alpha = jnp.exp2(m_prev - m_new)
    p = jnp.exp2(s - _lane_tile(m_new, tk // LANES))
    l_ref[c] = alpha * l_ref[c] + jnp.sum(p, axis=1, keepdims=True)
    acc_ref[c] = (acc_ref[c] * _lane_tile(alpha, dv // LANES)
                  + jnp.dot(p.astype(BF16), vb, preferred_element_type=F32))
    m_ref[c] = m_new


def _causal_mask(t):
    row = lax.broadcasted_iota(jnp.int32, (t, t), 0)
    col = lax.broadcasted_iota(jnp.int32, (t, t), 1)
    return col <= row


def _diff_attn_kernel(lq1_ref, lk1_ref, lq2_ref, lk2_ref, subg_ref, q0_ref, q1_ref, k0_ref, k1_ref,
                      v_ref, g_ref, o_ref, m_ref, l_ref, acc_ref, *, lam_init):
    i = pl.program_id(1)
    t = q0_ref.shape[0]
    dv = v_ref.shape[1]
    m_ref[...] = jnp.full_like(m_ref, NEG_BIG)
    l_ref[...] = jnp.zeros_like(l_ref)
    acc_ref[...] = jnp.zeros_like(acc_ref)
    k_refs = (k0_ref, k1_ref)
    q_refs = (q0_ref, q1_ref)

    def step(j, mask):
        start = pl.multiple_of(j * t, t)
        vb = v_ref[pl.ds(start, t), :]
        for c in range(2):
            kb = k_refs[c][pl.ds(start, t), :]
            s = lax.dot_general(q_refs[c][...], kb, _NT, preferred_element_type=F32)
            if mask is not None:
                s = jnp.where(mask, s, NEG_BIG)
            _softmax_step(s, vb, m_ref, l_ref, acc_ref, c)

    def body(j, carry):
        step(j, None)
        return carry

    lax.fori_loop(0, i, body, 0)
    step(i, _causal_mask(t))

    lam = (jnp.exp(jnp.sum(lq1_ref[...] * lk1_ref[...], axis=1, keepdims=True))
           - jnp.exp(jnp.sum(lq2_ref[...] * lk2_ref[...], axis=1, keepdims=True)) + lam_init)
    reps = dv // LANES
    o = (acc_ref[0] * _lane_tile(1.0 / l_ref[0], reps)
         - lam * (acc_ref[1] * _lane_tile(1.0 / l_ref[1], reps)))
    o = o * lax.rsqrt(jnp.mean(o * o, axis=1, keepdims=True) + NORM_EPS) * subg_ref[...]
    o = o * (1.0 - lam_init)
    o_ref[...] = (o * g_ref[...].astype(F32)).astype(BF16)


def _diff_attn(p, lq1, lk1, lq2, lk2, sub_g, *, heads, lam_init):
    s = p.shape[0]
    t = _tile_rows(s, ATTN_TILE)
    dv = 2 * HEAD_DIM
    vec = pl.BlockSpec((1, HEAD_DIM), lambda h, i: (0, 0))
    kern = functools.partial(_diff_attn_kernel, lam_init=lam_init)
    return pl.pallas_call(
        kern,
        grid=(heads, s // t),
        in_specs=[vec, vec, vec, vec,
                  pl.BlockSpec((1, dv), lambda h, i: (0, 0)),
                  pl.BlockSpec((t, HEAD_DIM), lambda h, i: (i, h)),
                  pl.BlockSpec((t, HEAD_DIM), lambda h, i: (i, heads + h)),
                  pl.BlockSpec((s, HEAD_DIM), lambda h, i: (0, 2 * heads + h)),
                  pl.BlockSpec((s, HEAD_DIM), lambda h, i: (0, 3 * heads + h)),
                  pl.BlockSpec((s, dv), lambda h, i: (0, 2 * heads + h)),
                  pl.BlockSpec((t, dv), lambda h, i: (i, 3 * heads + h))],
        out_specs=pl.BlockSpec((t, dv), lambda h, i: (i, h)),
        out_shape=jax.ShapeDtypeStruct((s, heads * dv), BF16),
        scratch_shapes=[pltpu.VMEM((2, t, LANES), F32), pltpu.VMEM((2, t, LANES), F32),
                        pltpu.VMEM((2, t, dv), F32)],
        compiler_params=_params(2),
        name="diff_attn",
    )(lq1.reshape(1, -1), lk1.reshape(1, -1), lq2.reshape(1, -1), lk2.reshape(1, -1),
      sub_g.reshape(1, -1), p, p, p, p, p, p)


def _fox_attn_kernel(q_ref, k_ref, v_ref, g_ref, c_ref, o_ref, m_ref, l_ref, acc_ref):
    i = pl.program_id(1)
    t = q_ref.shape[0]
    m_ref[...] = jnp.full_like(m_ref, NEG_BIG)
    l_ref[...] = jnp.zeros_like(l_ref)
    acc_ref[...] = jnp.zeros_like(acc_ref)

    def step(j, mask):
        start = pl.multiple_of(j * t, t)
        kb = k_ref[pl.ds(start, t), :]
        vb = v_ref[pl.ds(start, t), :]
        s = lax.dot_general(q_ref[...], kb, _NT, preferred_element_type=F32) + c_ref[0, j]
        if mask is not None:
            s = jnp.where(mask, s, NEG_BIG)
        _softmax_step(s, vb, m_ref, l_ref, acc_ref, 0)

    def body(j, carry):
        step(j, None)
        return carry

    lax.fori_loop(0, i, body, 0)
    step(i, _causal_mask(t))
    o = acc_ref[0] * (1.0 / l_ref[0])
    o_ref[...] = (o * g_ref[...].astype(F32)).astype(BF16)


def _fox_attn(p, cneg, *, heads):
    s = p.shape[0]
    t = _tile_rows(s, ATTN_TILE)
    nk = s // t
    return pl.pallas_call(
        _fox_attn_kernel,
        grid=(heads, s // t),
        in_specs=[pl.BlockSpec((t, HEAD_DIM), lambda h, i: (i, h)),
                  pl.BlockSpec((s, HEAD_DIM), lambda h, i: (0, heads + h)),
                  pl.BlockSpec((s, HEAD_DIM), lambda h, i: (0, 2 * heads + h)),
                  pl.BlockSpec((t, HEAD_DIM), lambda h, i: (i, 3 * heads + h)),
                  pl.BlockSpec((1, nk, 1, t), lambda h, i: (h, 0, 0, 0))],
        out_specs=pl.BlockSpec((t, HEAD_DIM), lambda h, i: (i, h)),
        out_shape=jax.ShapeDtypeStruct((s, heads * HEAD_DIM), BF16),
        scratch_shapes=[pltpu.VMEM((1, t, LANES), F32), pltpu.VMEM((1, t, LANES), F32),
                        pltpu.VMEM((1, t, HEAD_DIM), F32)],
        compiler_params=_params(2),
        name="fox_attn",
    )(p, p, p, p, cneg.reshape(heads, nk, 1, t))


def _mem_kv_kernel(mem_ref, g_ref, w_ref, kg_ref, o_ref, *, mem_hd, key_tiles):
    n = pl.program_id(1)
    x = mem_ref[...]
    mem_n = (x * lax.rsqrt(jnp.mean(x * x, axis=-1, keepdims=True) + NORM_EPS) * g_ref[...]).astype(BF16)
    kv = jnp.dot(mem_n, w_ref[0].astype(BF16), preferred_element_type=F32)
    tn = kv.shape[1]

    @pl.when(n < key_tiles)
    def _():
        for hh in range(tn // mem_hd):
            sl = slice(hh * mem_hd, (hh + 1) * mem_hd)
            t = kv[:, sl]
            t = t * lax.rsqrt(jnp.mean(t * t, axis=-1, keepdims=True) + NORM_EPS) * kg_ref[0]
            o_ref[0, :, sl] = t.astype(BF16)

    @pl.when(n >= key_tiles)
    def _():
        o_ref[0] = kv.astype(BF16)


def _mem_kv(mem, mem_g, w_mem_kv, k_g):
    n_mem, d = mem.shape
    depth, _, two_w = w_mem_kv.shape
    mem_width = two_w // 2
    mem_hd = mem_width // MEM_HEADS
    tn = mem_width // 2
    kern = functools.partial(_mem_kv_kernel, mem_hd=mem_hd, key_tiles=mem_width // tn)
    return pl.pallas_call(
        kern,
        grid=(depth, two_w // tn),
        in_specs=[pl.BlockSpec((n_mem, d), lambda l, n: (0, 0)),
                  pl.BlockSpec((1, d), lambda l, n: (0, 0)),
                  pl.BlockSpec((1, d, tn), lambda l, n: (l, 0, n)),
                  pl.BlockSpec((1, 1, mem_hd), lambda l, n: (l, 0, 0))],
        out_specs=pl.BlockSpec((1, n_mem, tn), lambda l, n: (l, 0, n)),
        out_shape=jax.ShapeDtypeStruct((depth, n_mem, two_w), BF16),
        compiler_params=_params(2),
        name="mem_kv",
    )(mem, mem_g.reshape(1, d), w_mem_kv, k_g.reshape(depth, 1, mem_hd))


def _mem_attn_kernel(q_ref, g_ref, km_ref, vm_ref, o_ref, *, mem_hd):
    for hh in range(MEM_HEADS):
        sl = slice(hh * mem_hd, (hh + 1) * mem_hd)
        s = lax.dot_general(q_ref[:, sl], km_ref[0, :, sl], _NT, preferred_element_type=F32)
        p = jnp.exp2(s - jnp.max(s, axis=1, keepdims=True))
        inv = 1.0 / jnp.sum(p, axis=1, keepdims=True)
        o = jnp.dot(p.astype(BF16), vm_ref[0, :, sl], preferred_element_type=F32) * inv
        o_ref[:, sl] = (o * g_ref[:, sl].astype(F32)).astype(BF16)


def _mem_attn(p, kvn, layer, *, mem_width):
    s, n_out = p.shape
    n_mem = kvn.shape[1]
    tm = _tile_rows(s, 512)
    q_blk = n_out // mem_width - 2
    kern = functools.partial(_mem_attn_kernel, mem_hd=mem_width // MEM_HEADS)
    return pl.pallas_call(
        kern,
        grid=(s // tm,),
        in_specs=[pl.BlockSpec((tm, mem_width), lambda i: (i, q_blk)),
                  pl.BlockSpec((tm, mem_width), lambda i: (i, q_blk + 1)),
                  pl.BlockSpec((1, n_mem, mem_width), lambda i: (layer, 0, 0)),
                  pl.BlockSpec((1, n_mem, mem_width), lambda i: (layer, 0, 1))],
        out_specs=pl.BlockSpec((tm, mem_width), lambda i: (i, 0)),
        out_shape=jax.ShapeDtypeStruct((s, mem_width), BF16),
        compiler_params=_params(1),
        name="mem_attn",
    )(p, p, kvn, kvn)


def _out_proj_kernel(x_ref, mix_ref, memo_ref, wa_ref, wb_ref, o_ref):
    o_ref[...] = (x_ref[...]
                  + jnp.dot(mix_ref[...], wa_ref[...], preferred_element_type=F32)
                  + jnp.dot(memo_ref[...], wb_ref[...], preferred_element_type=F32))


def _out_proj(x, mix, memo, w):
    s, d = x.shape
    mix_w = mix.shape[1]
    mem_w = memo.shape[1]
    assert mix_w % mem_w == 0
    tm = _tile_rows(s, 512)
    tn = _tile_rows(d, 1024)
    return pl.pallas_call(
        _out_proj_kernel,
        grid=(s // tm, d // tn),
        in_specs=[pl.BlockSpec((tm, tn), lambda i, n: (i, n)),
                  pl.BlockSpec((tm, mix_w), lambda i, n: (i, 0)),
                  pl.BlockSpec((tm, mem_w), lambda i, n: (i, 0)),
                  pl.BlockSpec((mix_w, tn), lambda i, n: (0, n)),
                  pl.BlockSpec((mem_w, tn), lambda i, n: (mix_w // mem_w, n))],
        out_specs=pl.BlockSpec((tm, tn), lambda i, n: (i, n)),
        out_shape=jax.ShapeDtypeStruct((s, d), F32),
        compiler_params=_params(2),
        name="out_proj",
    )(x, mix, memo, w, w)


def _rope_tables(seq):
    half = ROT_DIM // 2
    inv = jnp.power(ROPE_THETA, -jnp.arange(0, ROT_DIM, 2, dtype=F32) / ROT_DIM)
    ang = jnp.arange(seq, dtype=F32)[:, None] * inv[None, :]
    cos, sin = jnp.cos(ang), jnp.sin(ang)
    zeros = jnp.zeros((seq, HEAD_DIM - ROT_DIM), F32)
    z_half = jnp.zeros((seq, half), F32)
    cos_t = jnp.concatenate([cos, cos, jnp.ones((seq, HEAD_DIM - ROT_DIM), F32)], axis=1)
    sa_t = jnp.concatenate([-sin, z_half, zeros], axis=1)
    sb_t = jnp.concatenate([z_half, sin, zeros], axis=1)
    return cos_t, sa_t, sb_t


def kernel(x, mem, mem_norm_g, ln_g, w_out, w_mem_kv, mem_q_norm_g, mem_k_norm_g, diff_w_in, diff_q_norm_g, diff_k_norm_g, diff_lam_q1, diff_lam_k1, diff_lam_q2, diff_lam_k2, diff_subln_g, fox_w_in, fox_b_f, fox_q_norm_g, fox_k_norm_g):
    b, s, d = x.shape
    assert b == 1
    depth = ln_g.shape[0]
    mem_width = w_mem_kv.shape[2] // 2
    mix_width = d - mem_width
    diff_heads = mix_width // (2 * HEAD_DIM)
    fox_heads = mix_width // HEAD_DIM

    rope_tabs = _rope_tables(s)
    kvn = _mem_kv(mem[0], mem_norm_g, w_mem_kv, mem_k_norm_g)
    xs = x[0]
    for i in range(depth):
        j = i // 2
        h = _rms_cast(xs, ln_g[i])
        if i % 2 == 0:
            lam_init = 0.8 - 0.6 * math.exp(-0.3 * i)
            p = _in_proj(h, diff_w_in[j].astype(BF16), diff_q_norm_g[j], diff_k_norm_g[j],
                         mem_q_norm_g[i], rope_tabs, rope=True, mem_width=mem_width)
            mix = _diff_attn(p, diff_lam_q1[j], diff_lam_k1[j], diff_lam_q2[j], diff_lam_k2[j],
                             diff_subln_g[j], heads=diff_heads, lam_init=lam_init)
        else:
            w = fox_w_in[j]
            n_main = 4 * mix_width
            w_main = jnp.concatenate([w[:, :n_main], w[:, n_main + fox_heads:]], axis=1).astype(BF16)
            w_fl = jnp.pad(w[:, n_main:n_main + fox_heads], ((0, 0), (0, LANES - fox_heads))).astype(BF16)
            b_f = jnp.pad(fox_b_f[j], (0, LANES - fox_heads)).reshape(1, LANES)
            p = _in_proj(h, w_main, fox_q_norm_g[j], fox_k_norm_g[j], mem_q_norm_g[i], rope_tabs,
                         rope=False, mem_width=mem_width)
            cneg = _fox_gates(h, w_fl, b_f)[:, :fox_heads].T
            mix = _fox_attn(p, cneg, heads=fox_heads)
        memo = _mem_attn(p, kvn, i, mem_width=mem_width)
        xs = _out_proj(xs, mix, memo, w_out[i].astype(BF16))
    return xs[None]
```

```python
import functools
import math

import jax
import jax.numpy as jnp
from jax import lax
from jax.experimental import pallas as pl
from jax.experimental.pallas import tpu as pltpu

HEAD_DIM = 128
MEM_HEADS = 4
ROT_DIM = HEAD_DIM // 4
ROPE_THETA = 500000.0
NORM_EPS = 1e-6
LOG2E = 1.4426950408889634
NEG_BIG = -1e30

LANES = 128
V7X_VMEM_BYTES = 64 * 1024 * 1024
VMEM_LIMIT = V7X_VMEM_BYTES - 8 * 1024 * 1024
ATTN_TILE = 512
IN_PROJ_CHUNKS = 2

F32 = jnp.float32
BF16 = jnp.bfloat16
_NT = (((1,), (1,)), ((), ()))


def _params(n_axes):
    return pltpu.CompilerParams(
        dimension_semantics=("arbitrary",) * n_axes, vmem_limit_bytes=VMEM_LIMIT)


def _tile_rows(n, target):
    t = min(n, target)
    assert n % t == 0
    return t


def _lane_tile(x, reps):
    return x if reps == 1 else jnp.concatenate([x] * reps, axis=1)


def _silu(x):
    return x / (1.0 + jnp.exp(-x))


def _rms_cast_kernel(x_ref, g_ref, o_ref):
    x = x_ref[...]
    y = x * lax.rsqrt(jnp.mean(x * x, axis=-1, keepdims=True) + NORM_EPS)
    o_ref[...] = (y * g_ref[...]).astype(BF16)


def _rms_cast(x, g):
    s, d = x.shape
    tm = _tile_rows(s, 256)
    return pl.pallas_call(
        _rms_cast_kernel,
        grid=(s // tm,),
        in_specs=[pl.BlockSpec((tm, d), lambda i: (i, 0)),
                  pl.BlockSpec((1, d), lambda i: (0, 0))],
        out_specs=pl.BlockSpec((tm, d), lambda i: (i, 0)),
        out_shape=jax.ShapeDtypeStruct((s, d), BF16),
        compiler_params=_params(1),
        name="rms_cast",
    )(x, g.reshape(1, d))


def _in_proj_kernel(h_ref, w_ref, gq_ref, gk_ref, gm_ref, cos_ref, sa_ref, sb_ref, o_ref,
                    *, rope, mix_tiles, q_scale, mq_scale, mem_hd):
    n = pl.program_id(1)
    tn = o_ref.shape[1]
    chunk = tn // IN_PROJ_CHUNKS

    def chunks(epilogue):
        for cc in range(IN_PROJ_CHUNKS):
            cols = slice(cc * chunk, (cc + 1) * chunk)
            acc = jnp.dot(h_ref[...], w_ref[:, cols], preferred_element_type=F32)
            o_ref[:, cols] = epilogue(acc).astype(BF16)

    def head_norm(acc, gain, width):
        outs = []
        for hh in range(chunk // width):
            t = acc[:, hh * width:(hh + 1) * width]
            outs.append(t * lax.rsqrt(jnp.mean(t * t, axis=-1, keepdims=True) + NORM_EPS) * gain)
        return outs

    def qk_epilogue(gain):
        def epilogue(acc):
            outs = head_norm(acc, gain, HEAD_DIM)
            if rope:
                outs = [t * cos_ref[...] + pltpu.roll(t, HEAD_DIM - ROT_DIM // 2, 1) * sa_ref[...]
                        + pltpu.roll(t, ROT_DIM // 2, 1) * sb_ref[...] for t in outs]
            return jnp.concatenate(outs, axis=1)
        return epilogue

    @pl.when(n < mix_tiles)
    def _():
        chunks(qk_epilogue(gq_ref[...] * q_scale))

    @pl.when(jnp.logical_and(n >= mix_tiles, n < 2 * mix_tiles))
    def _():
        chunks(qk_epilogue(gk_ref[...]))

    @pl.when(jnp.logical_and(n >= 2 * mix_tiles, n < 3 * mix_tiles))
    def _():
        chunks(lambda acc: acc)

    @pl.when(jnp.logical_or(jnp.logical_and(n >= 3 * mix_tiles, n < 4 * mix_tiles), n == 4 * mix_tiles + 1))
    def _():
        chunks(_silu)

    @pl.when(n == 4 * mix_tiles)
    def _():
        gain = gm_ref[...] * mq_scale
        chunks(lambda acc: jnp.concatenate(head_norm(acc, gain, mem_hd), axis=1))


def _in_proj(h, w, gq, gk, gm, rope_tabs, *, rope, mem_width):
    s, d = h.shape
    n_out = w.shape[1]
    tn = mem_width
    mem_hd = mem_width // MEM_HEADS
    mix_tiles = (d - mem_width) // tn
    assert n_out == (4 * mix_tiles + 2) * tn
    tm = _tile_rows(s, 512)
    cos_t, sa_t, sb_t = rope_tabs
    kern = functools.partial(
        _in_proj_kernel, rope=rope, mix_tiles=mix_tiles,
        q_scale=HEAD_DIM ** -0.5 * LOG2E, mq_scale=mem_hd ** -0.5 * LOG2E, mem_hd=mem_hd)
    row_tab = pl.BlockSpec((tm, HEAD_DIM), lambda i, n: (i, 0))
    return pl.pallas_call(
        kern,
        grid=(s // tm, n_out // tn),
        in_specs=[pl.BlockSpec((tm, d), lambda i, n: (i, 0)),
                  pl.BlockSpec((d, tn), lambda i, n: (0, n)),
                  pl.BlockSpec((1, HEAD_DIM), lambda i, n: (0, 0)),
                  pl.BlockSpec((1, HEAD_DIM), lambda i, n: (0, 0)),
                  pl.BlockSpec((1, mem_hd), lambda i, n: (0, 0)),
                  row_tab, row_tab, row_tab],
        out_specs=pl.BlockSpec((tm, tn), lambda i, n: (i, n)),
        out_shape=jax.ShapeDtypeStruct((s, n_out), BF16),
        compiler_params=_params(2),
        name="in_proj",
    )(h, w, gq.reshape(1, -1), gk.reshape(1, -1), gm.reshape(1, -1), cos_t, sa_t, sb_t)


def _fox_gate_kernel(h_ref, w_ref, b_ref, o_ref, carry_ref):
    @pl.when(pl.program_id(0) == 0)
    def _():
        carry_ref[...] = jnp.zeros_like(carry_ref)

    tm = h_ref.shape[0]
    fl = jnp.dot(h_ref[...], w_ref[...], preferred_element_type=F32) + b_ref[...]
    ls = jnp.minimum(fl, 0.0) - jnp.log1p(jnp.exp(-jnp.abs(fl)))
    hi = ls.astype(BF16)
    r1 = ls - hi.astype(F32)
    mid = r1.astype(BF16)
    lo = (r1 - mid.astype(F32)).astype(BF16)
    row = lax.broadcasted_iota(jnp.int32, (tm, tm), 0)
    col = lax.broadcasted_iota(jnp.int32, (tm, tm), 1)
    tri = jnp.where(row >= col, 1.0, 0.0).astype(BF16)
    c = (jnp.dot(tri, hi, preferred_element_type=F32)
         + jnp.dot(tri, mid, preferred_element_type=F32)
         + jnp.dot(tri, lo, preferred_element_type=F32)) + carry_ref[...]
    carry_ref[...] = c[tm - 1:tm, :]
    o_ref[...] = c * (-LOG2E)


def _fox_gates(h, w_fl, b_f):
    s, d = h.shape
    tm = _tile_rows(s, 512)
    return pl.pallas_call(
        _fox_gate_kernel,
        grid=(s // tm,),
        in_specs=[pl.BlockSpec((tm, d), lambda i: (i, 0)),
                  pl.BlockSpec((d, LANES), lambda i: (0, 0)),
                  pl.BlockSpec((1, LANES), lambda i: (0, 0))],
        out_specs=pl.BlockSpec((tm, LANES), lambda i: (i, 0)),
        out_shape=jax.ShapeDtypeStruct((s, LANES), F32),
        scratch_shapes=[pltpu.VMEM((1, LANES), F32)],
        compiler_params=_params(1),
        name="fox_gates",
    )(h, w_fl, b_f)


def _softmax_stage(s_ref, p_ref, alpha_ref, m_ref, l_ref, par, c, masked):
    tq, tk = s_ref.shape[1:]
    s = s_ref[par]
    if masked:
        row = lax.broadcasted_iota(jnp.int32, (tq, tk), 0)
        col = lax.broadcasted_iota(jnp.int32, (tq, tk), 1)
        s = jnp.where(col <= row, s, NEG_BIG)
    m_prev = m_ref[c]
    m_new = jnp.maximum(m_prev, jnp.max(s, axis=1, keepdims=True))
    alpha = jnp.exp2(m_prev - m_new)
    p = jnp.exp2(s - _lane_tile(m_new, tk // LANES))
    l_ref[c] = alpha * l_ref[c] + jnp.sum(p, axis=1, keepdims=True)
    m_ref[c] = m_new
    alpha_ref[par] = alpha
    p_ref[par] = p.astype(BF16)


def _value_stage(p_ref, alpha_ref, acc_ref, vb, par, c):
    dv = vb.shape[1]
    acc_ref[c] = (acc_ref[c] * _lane_tile(alpha_ref[par], dv // LANES)
                  + jnp.dot(p_ref[par], vb, preferred_element_type=F32))


def _causal_sweep(i, scores, softmax, values):
    scores(i, 0)

    @pl.when(i == 0)
    def _():
        softmax(0, True)
        values(i, 0)

    @pl.when(i > 0)
    def _():
        scores(0, 1)
        softmax(0, True)

        def pair(tt, carry):
            scores(2 * tt + 1, 0)
            softmax(1, False)
            values(jnp.where(tt == 0, i, 2 * tt - 1), 0)
            scores(2 * tt + 2, 1)
            softmax(0, False)
            values(2 * tt, 1)
            return carry

        lax.fori_loop(0, lax.shift_right_logical(i - 1, 1), pair, 0)
        odd = jnp.bitwise_and(i, 1)

        @pl.when(odd == 1)
        def _():
            softmax(1, False)
            values(jnp.where(i == 1, i, i - 2), 0)
            values(i - 1, 1)

        @pl.when(odd == 0)
        def _():
            scores(i - 1, 0)
            softmax(1, False)
            values(jnp.where(i == 2, i, i - 3), 0)
            softmax(0, False)
            values(i - 2, 1)
            values(i - 1, 0)


def _diff_attn_kernel(lq1_ref, lk1_ref, lq2_ref, lk2_ref, subg_ref, q0_ref, q1_ref, k0_ref, k1_ref,
                      v_ref, g_ref, o_ref, m_ref, l_ref, acc_ref, s_ref, p_ref, alpha_ref, *, lam_init):
    i = pl.program_id(1)
    t = q0_ref.shape[0]
    dv = v_ref.shape[1]
    m_ref[...] = jnp.full_like(m_ref, NEG_BIG)
    l_ref[...] = jnp.zeros_like(l_ref)
    acc_ref[...] = jnp.zeros_like(acc_ref)
    k_refs = (k0_ref, k1_ref)
    q_refs = (q0_ref, q1_ref)

    for c in range(2):
        def scores(kb, par, c=c):
            start = pl.multiple_of(kb * t, t)
            s_ref[par] = lax.dot_general(q_refs[c][...], k_refs[c][pl.ds(start, t), :], _NT,
                                         preferred_element_type=F32)

        def softmax(par, masked, c=c):
            _softmax_stage(s_ref, p_ref, alpha_ref, m_ref, l_ref, par, c, masked)

        def values(kb, par, c=c):
            start = pl.multiple_of(kb * t, t)
            _value_stage(p_ref, alpha_ref, acc_ref, v_ref[pl.ds(start, t), :], par, c)

        _causal_sweep(i, scores, softmax, values)

    lam = (jnp.exp(jnp.sum(lq1_ref[...] * lk1_ref[...], axis=1, keepdims=True))
           - jnp.exp(jnp.sum(lq2_ref[...] * lk2_ref[...], axis=1, keepdims=True)) + lam_init)
    reps = dv // LANES
    o = (acc_ref[0] * _lane_tile(1.0 / l_ref[0], reps)
         - lam * (acc_ref[1] * _lane_tile(1.0 / l_ref[1], reps)))
    o = o * lax.rsqrt(jnp.mean(o * o, axis=1, keepdims=True) + NORM_EPS) * subg_ref[...]
    o = o * (1.0 - lam_init)
    o_ref[...] = (o * g_ref[...].astype(F32)).astype(BF16)


def _attn_scratch(n_comp, t, dv):
    return [pltpu.VMEM((n_comp, t, LANES), F32), pltpu.VMEM((n_comp, t, LANES), F32),
            pltpu.VMEM((n_comp, t, dv), F32),
            pltpu.VMEM((2, t, t), F32), pltpu.VMEM((2, t, t), BF16), pltpu.VMEM((2, t, LANES), F32)]


def _diff_attn(p, lq1, lk1, lq2, lk2, sub_g, *, heads, lam_init):
    s = p.shape[0]
    t = _tile_rows(s, ATTN_TILE)
    dv = 2 * HEAD_DIM
    vec = pl.BlockSpec((1, HEAD_DIM), lambda h, i: (0, 0))
    kern = functools.partial(_diff_attn_kernel, lam_init=lam_init)
    return pl.pallas_call(
        kern,
        grid=(heads, s // t),
        in_specs=[vec, vec, vec, vec,
                  pl.BlockSpec((1, dv), lambda h, i: (0, 0)),
                  pl.BlockSpec((t, HEAD_DIM), lambda h, i: (i, h)),
                  pl.BlockSpec((t, HEAD_DIM), lambda h, i: (i, heads + h)),
                  pl.BlockSpec((s, HEAD_DIM), lambda h, i: (0, 2 * heads + h)),
                  pl.BlockSpec((s, HEAD_DIM), lambda h, i: (0, 3 * heads + h)),
                  pl.BlockSpec((s, dv), lambda h, i: (0, 2 * heads + h)),
                  pl.BlockSpec((t, dv), lambda h, i: (i, 3 * heads + h))],
        out_specs=pl.BlockSpec((t, dv), lambda h, i: (i, h)),
        out_shape=jax.ShapeDtypeStruct((s, heads * dv), BF16),
        scratch_shapes=_attn_scratch(2, t, dv),
        compiler_params=_params(2),
        name="diff_attn",
    )(lq1.reshape(1, -1), lk1.reshape(1, -1), lq2.reshape(1, -1), lk2.reshape(1, -1),
      sub_g.reshape(1, -1), p, p, p, p, p, p)


def _fox_attn_kernel(q_ref, k_ref, v_ref, g_ref, c_ref, o_ref, m_ref, l_ref, acc_ref, s_ref, p_ref, alpha_ref):
    i = pl.program_id(1)
    t = q_ref.shape[0]
    m_ref[...] = jnp.full_like(m_ref, NEG_BIG)
    l_ref[...] = jnp.zeros_like(l_ref)
    acc_ref[...] = jnp.zeros_like(acc_ref)

    def scores(kb, par):
        start = pl.multiple_of(kb * t, t)
        s_ref[par] = (lax.dot_general(q_ref[...], k_ref[pl.ds(start, t), :], _NT,
                                      preferred_element_type=F32) + c_ref[0, kb])

    def softmax(par, masked):
        _softmax_stage(s_ref, p_ref, alpha_ref, m_ref, l_ref, par, 0, masked)

    def values(kb, par):
        start = pl.multiple_of(kb * t, t)
        _value_stage(p_ref, alpha_ref, acc_ref, v_ref[pl.ds(start, t), :], par, 0)

    _causal_sweep(i, scores, softmax, values)
    o = acc_ref[0] * (1.0 / l_ref[0])
    o_ref[...] = (o * g_ref[...].astype(F32)).astype(BF16)


def _fox_attn(p, cneg, *, heads):
    s = p.shape[0]
    t = _tile_rows(s, ATTN_TILE)
    nk = s // t
    return pl.pallas_call(
        _fox_attn_kernel,
        grid=(heads, s // t),
        in_specs=[pl.BlockSpec((t, HEAD_DIM), lambda h, i: (i, h)),
                  pl.BlockSpec((s, HEAD_DIM), lambda h, i: (0, heads + h)),
                  pl.BlockSpec((s, HEAD_DIM), lambda h, i: (0, 2 * heads + h)),
                  pl.BlockSpec((t, HEAD_DIM), lambda h, i: (i, 3 * heads + h)),
                  pl.BlockSpec((1, nk, 1, t), lambda h, i: (h, 0, 0, 0))],
        out_specs=pl.BlockSpec((t, HEAD_DIM), lambda h, i: (i, h)),
        out_shape=jax.ShapeDtypeStruct((s, heads * HEAD_DIM), BF16),
        scratch_shapes=_attn_scratch(1, t, HEAD_DIM),
        compiler_params=_params(2),
        name="fox_attn",
    )(p, p, p, p, cneg.reshape(heads, nk, 1, t))


def _mem_kv_kernel(mem_ref, g_ref, w_ref, kg_ref, o_ref, *, mem_hd, key_tiles):
    n = pl.program_id(1)
    x = mem_ref[...]
    mem_n = (x * lax.rsqrt(jnp.mean(x * x, axis=-1, keepdims=True) + NORM_EPS) * g_ref[...]).astype(BF16)
    kv = jnp.dot(mem_n, w_ref[0].astype(BF16), preferred_element_type=F32)
    tn = kv.shape[1]

    @pl.when(n < key_tiles)
    def _():
        for hh in range(tn // mem_hd):
            sl = slice(hh * mem_hd, (hh + 1) * mem_hd)
            t = kv[:, sl]
            t = t * lax.rsqrt(jnp.mean(t * t, axis=-1, keepdims=True) + NORM_EPS) * kg_ref[0]
            o_ref[0, :, sl] = t.astype(BF16)

    @pl.when(n >= key_tiles)
    def _():
        o_ref[0] = kv.astype(BF16)


def _mem_kv(mem, mem_g, w_mem_kv, k_g):
    n_mem, d = mem.shape
    depth, _, two_w = w_mem_kv.shape
    mem_width = two_w // 2
    mem_hd = mem_width // MEM_HEADS
    tn = mem_width // 2
    kern = functools.partial(_mem_kv_kernel, mem_hd=mem_hd, key_tiles=mem_width // tn)
    return pl.pallas_call(
        kern,
        grid=(depth, two_w // tn),
        in_specs=[pl.BlockSpec((n_mem, d), lambda l, n: (0, 0)),
                  pl.BlockSpec((1, d), lambda l, n: (0, 0)),
                  pl.BlockSpec((1, d, tn), lambda l, n: (l, 0, n)),
                  pl.BlockSpec((1, 1, mem_hd), lambda l, n: (l, 0, 0))],
        out_specs=pl.BlockSpec((1, n_mem, tn), lambda l, n: (l, 0, n)),
        out_shape=jax.ShapeDtypeStruct((depth, n_mem, two_w), BF16),
        compiler_params=_params(2),
        name="mem_kv",
    )(mem, mem_g.reshape(1, d), w_mem_kv, k_g.reshape(depth, 1, mem_hd))


def _mem_attn_kernel(q_ref, g_ref, km_ref, vm_ref, o_ref, *, mem_hd):
    for hh in range(MEM_HEADS):
        sl = slice(hh * mem_hd, (hh + 1) * mem_hd)
        s = lax.dot_general(q_ref[:, sl], km_ref[0, :, sl], _NT, preferred_element_type=F32)
        p = jnp.exp2(s - jnp.max(s, axis=1, keepdims=True))
        inv = 1.0 / jnp.sum(p, axis=1, keepdims=True)
        o = jnp.dot(p.astype(BF16), vm_ref[0, :, sl], preferred_element_type=F32) * inv
        o_ref[:, sl] = (o * g_ref[:, sl].astype(F32)).astype(BF16)


def _mem_attn(p, kvn, layer, *, mem_width):
    s, n_out = p.shape
    n_mem = kvn.shape[1]
    tm = _tile_rows(s, 512)
    q_blk = n_out // mem_width - 2
    kern = functools.partial(_mem_attn_kernel, mem_hd=mem_width // MEM_HEADS)
    return pl.pallas_call(
        kern,
        grid=(s // tm,),
        in_specs=[pl.BlockSpec((tm, mem_width), lambda i: (i, q_blk)),
                  pl.BlockSpec((tm, mem_width), lambda i: (i, q_blk + 1)),
                  pl.BlockSpec((1, n_mem, mem_width), lambda i: (layer, 0, 0)),
                  pl.BlockSpec((1, n_mem, mem_width), lambda i: (layer, 0, 1))],
        out_specs=pl.BlockSpec((tm, mem_width), lambda i: (i, 0)),
        out_shape=jax.ShapeDtypeStruct((s, mem_width), BF16),
        compiler_params=_params(1),
        name="mem_attn",
    )(p, p, kvn, kvn)


def _out_proj_kernel(x_ref, mix_ref, memo_ref, wa_ref, wb_ref, o_ref):
    o_ref[...] = (x_ref[...]
                  + jnp.dot(mix_ref[...], wa_ref[...], preferred_element_type=F32)
                  + jnp.dot(memo_ref[...], wb_ref[...], preferred_element_type=F32))


def _out_proj(x, mix, memo, w):
    s, d = x.shape
    mix_w = mix.shape[1]
    mem_w = memo.shape[1]
    assert mix_w % mem_w == 0
    tm = _tile_rows(s, 512)
    tn = _tile_rows(d, 1024)
    return pl.pallas_call(
        _out_proj_kernel,
        grid=(s // tm, d // tn),
        in_specs=[pl.BlockSpec((tm, tn), lambda i, n: (i, n)),
                  pl.BlockSpec((tm, mix_w), lambda i, n: (i, 0)),
                  pl.BlockSpec((tm, mem_w), lambda i, n: (i, 0)),
                  pl.BlockSpec((mix_w, tn), lambda i, n: (0, n)),
                  pl.BlockSpec((mem_w, tn), lambda i, n: (mix_w // mem_w, n))],
        out_specs=pl.BlockSpec((tm, tn), lambda i, n: (i, n)),
        out_shape=jax.ShapeDtypeStruct((s, d), F32),
        compiler_params=_params(2),
        name="out_proj",
    )(x, mix, memo, w, w)


def _rope_tables(seq):
    half = ROT_DIM // 2
    inv = jnp.power(ROPE_THETA, -jnp.arange(0, ROT_DIM, 2, dtype=F32) / ROT_DIM)
    ang = jnp.arange(seq, dtype=F32)[:, None] * inv[None, :]
    cos, sin = jnp.cos(ang), jnp.sin(ang)
    zeros = jnp.zeros((seq, HEAD_DIM - ROT_DIM), F32)
    z_half = jnp.zeros((seq, half), F32)
    cos_t = jnp.concatenate([cos, cos, jnp.ones((seq, HEAD_DIM - ROT_DIM), F32)], axis=1)
    sa_t = jnp.concatenate([-sin, z_half, zeros], axis=1)
    sb_t = jnp.concatenate([z_half, sin, zeros], axis=1)
    return cos_t, sa_t, sb_t


def kernel(x, mem, mem_norm_g, ln_g, w_out, w_mem_kv, mem_q_norm_g, mem_k_norm_g, diff_w_in, diff_q_norm_g, diff_k_norm_g, diff_lam_q1, diff_lam_k1, diff_lam_q2, diff_lam_k2, diff_subln_g, fox_w_in, fox_b_f, fox_q_norm_g, fox_k_norm_g):
    b, s, d = x.shape
    assert b == 1
    depth = ln_g.shape[0]
    mem_width = w_mem_kv.shape[2] // 2
    mix_width = d - mem_width
    diff_heads = mix_width // (2 * HEAD_DIM)
    fox_heads = mix_width // HEAD_DIM

    rope_tabs = _rope_tables(s)
    kvn = _mem_kv(mem[0], mem_norm_g, w_mem_kv, mem_k_norm_g)
    xs = x[0]
    for i in range(depth):
        j = i // 2
        h = _rms_cast(xs, ln_g[i])
        if i % 2 == 0:
            lam_init = 0.8 - 0.6 * math.exp(-0.3 * i)
            p = _in_proj(h, diff_w_in[j].astype(BF16), diff_q_norm_g[j], diff_k_norm_g[j],
                         mem_q_norm_g[i], rope_tabs, rope=True, mem_width=mem_width)
            mix = _diff_attn(p, diff_lam_q1[j], diff_lam_k1[j], diff_lam_q2[j], diff_lam_k2[j],
                             diff_subln_g[j], heads=diff_heads, lam_init=lam_init)
        else:
            w = fox_w_in[j]
            n_main = 4 * mix_width
            w_main = jnp.concatenate([w[:, :n_main], w[:, n_main + fox_heads:]], axis=1).astype(BF16)
            w_fl = jnp.pad(w[:, n_main:n_main + fox_heads], ((0, 0), (0, LANES - fox_heads))).astype(BF16)
            b_f = jnp.pad(fox_b_f[j], (0, LANES - fox_heads)).reshape(1, LANES)
            p = _in_proj(h, w_main, fox_q_norm_g[j], fox_k_norm_g[j], mem_q_norm_g[i], rope_tabs,
                         rope=False, mem_width=mem_width)
            cneg = _fox_gates(h, w_fl, b_f)[:, :fox_heads].T
            mix = _fox_attn(p, cneg, heads=fox_heads)
        memo = _mem_attn(p, kvn, i, mem_width=mem_width)
        xs = _out_proj(xs, mix, memo, w_out[i].astype(BF16))
    return xs[None]
```

```python
import functools
import math

import jax
import jax.numpy as jnp
from jax import lax
from jax.experimental import pallas as pl
from jax.experimental.pallas import tpu as pltpu

HEAD_DIM = 128
MEM_HEADS = 4
ROT_DIM = HEAD_DIM // 4
ROPE_THETA = 500000.0
NORM_EPS = 1e-6
LOG2E = 1.4426950408889634
NEG_BIG = -1e30

LANES = 128
V7X_VMEM_BYTES = 64 * 1024 * 1024
VMEM_LIMIT = V7X_VMEM_BYTES - 8 * 1024 * 1024
ATTN_TILE = 512
IN_PROJ_CHUNKS = 2

F32 = jnp.float32
BF16 = jnp.bfloat16
_NT = (((1,), (1,)), ((), ()))


def _params(n_axes):
    return pltpu.CompilerParams(
        dimension_semantics=("arbitrary",) * n_axes, vmem_limit_bytes=VMEM_LIMIT)


def _tile_rows(n, target):
    t = min(n, target)
    assert n % t == 0
    return t


def _lane_tile(x, reps):
    return x if reps == 1 else jnp.concatenate([x] * reps, axis=1)


def _silu(x):
    return x / (1.0 + jnp.exp(-x))


def _rms_cast_kernel(x_ref, g_ref, o_ref):
    x = x_ref[...]
    y = x * lax.rsqrt(jnp.mean(x * x, axis=-1, keepdims=True) + NORM_EPS)
    o_ref[...] = (y * g_ref[...]).astype(BF16)


def _rms_cast(x, g):
    s, d = x.shape
    tm = _tile_rows(s, 256)
    return pl.pallas_call(
        _rms_cast_kernel,
        grid=(s // tm,),
        in_specs=[pl.BlockSpec((tm, d), lambda i: (i, 0)),
                  pl.BlockSpec((1, d), lambda i: (0, 0))],
        out_specs=pl.BlockSpec((tm, d), lambda i: (i, 0)),
        out_shape=jax.ShapeDtypeStruct((s, d), BF16),
        compiler_params=_params(1),
        name="rms_cast",
    )(x, g.reshape(1, d))


def _in_proj_kernel(h_ref, w_ref, gq_ref, gk_ref, gm_ref, cos_ref, sa_ref, sb_ref, o_ref,
                    *, rope, mix_tiles, q_scale, mq_scale, mem_hd):
    n = pl.program_id(1)
    tn = o_ref.shape[1]
    chunk = tn // IN_PROJ_CHUNKS

    def chunks(epilogue):
        for cc in range(IN_PROJ_CHUNKS):
            cols = slice(cc * chunk, (cc + 1) * chunk)
            acc = jnp.dot(h_ref[...], w_ref[:, cols], preferred_element_type=F32)
            o_ref[:, cols] = epilogue(acc).astype(BF16)

    def head_norm(acc, gain, width):
        outs = []
        for hh in range(chunk // width):
            t = acc[:, hh * width:(hh + 1) * width]
            outs.append(t * lax.rsqrt(jnp.mean(t * t, axis=-1, keepdims=True) + NORM_EPS) * gain)
        return outs

    def qk_epilogue(gain):
        def epilogue(acc):
            outs = head_norm(acc, gain, HEAD_DIM)
            if rope:
                outs = [t * cos_ref[...] + pltpu.roll(t, HEAD_DIM - ROT_DIM // 2, 1) * sa_ref[...]
                        + pltpu.roll(t, ROT_DIM // 2, 1) * sb_ref[...] for t in outs]
            return jnp.concatenate(outs, axis=1)
        return epilogue

    @pl.when(n < mix_tiles)
    def _():
        chunks(qk_epilogue(gq_ref[...] * q_scale))

    @pl.when(jnp.logical_and(n >= mix_tiles, n < 2 * mix_tiles))
    def _():
        chunks(qk_epilogue(gk_ref[...]))

    @pl.when(jnp.logical_and(n >= 2 * mix_tiles, n < 3 * mix_tiles))
    def _():
        chunks(lambda acc: acc)

    @pl.when(jnp.logical_or(jnp.logical_and(n >= 3 * mix_tiles, n < 4 * mix_tiles), n == 4 * mix_tiles + 1))
    def _():
        chunks(_silu)

    @pl.when(n == 4 * mix_tiles)
    def _():
        gain = gm_ref[...] * mq_scale
        chunks(lambda acc: jnp.concatenate(head_norm(acc, gain, mem_hd), axis=1))


def _in_proj(h, w, gq, gk, gm, rope_tabs, *, rope, mem_width):
    s, d = h.shape
    n_out = w.shape[1]
    tn = mem_width
    mem_hd = mem_width // MEM_HEADS
    mix_tiles = (d - mem_width) // tn
    assert n_out == (4 * mix_tiles + 2) * tn
    tm = _tile_rows(s, 512)
    cos_t, sa_t, sb_t = rope_tabs
    kern = functools.partial(
        _in_proj_kernel, rope=rope, mix_tiles=mix_tiles,
        q_scale=HEAD_DIM ** -0.5 * LOG2E, mq_scale=mem_hd ** -0.5 * LOG2E, mem_hd=mem_hd)
    row_tab = pl.BlockSpec((tm, HEAD_DIM), lambda i, n: (i, 0))
    return pl.pallas_call(
        kern,
        grid=(s // tm, n_out // tn),
        in_specs=[pl.BlockSpec((tm, d), lambda i, n: (i, 0)),
                  pl.BlockSpec((d, tn), lambda i, n: (0, n)),
                  pl.BlockSpec((1, HEAD_DIM), lambda i, n: (0, 0)),
                  pl.BlockSpec((1, HEAD_DIM), lambda i, n: (0, 0)),
                  pl.BlockSpec((1, mem_hd), lambda i, n: (0, 0)),
                  row_tab, row_tab, row_tab],
        out_specs=pl.BlockSpec((tm, tn), lambda i, n: (i, n)),
        out_shape=jax.ShapeDtypeStruct((s, n_out), BF16),
        compiler_params=_params(2),
        name="in_proj",
    )(h, w, gq.reshape(1, -1), gk.reshape(1, -1), gm.reshape(1, -1), cos_t, sa_t, sb_t)


def _fox_gate_kernel(h_ref, w_ref, b_ref, o_ref, carry_ref):
    @pl.when(pl.program_id(0) == 0)
    def _():
        carry_ref[...] = jnp.zeros_like(carry_ref)

    tm = h_ref.shape[0]
    fl = jnp.dot(h_ref[...], w_ref[...], preferred_element_type=F32) + b_ref[...]
    ls = jnp.minimum(fl, 0.0) - jnp.log1p(jnp.exp(-jnp.abs(fl)))
    hi = ls.astype(BF16)
    r1 = ls - hi.astype(F32)
    mid = r1.astype(BF16)
    lo = (r1 - mid.astype(F32)).astype(BF16)
    row = lax.broadcasted_iota(jnp.int32, (tm, tm), 0)
    col = lax.broadcasted_iota(jnp.int32, (tm, tm), 1)
    tri = jnp.where(row >= col, 1.0, 0.0).astype(BF16)
    c = (jnp.dot(tri, hi, preferred_element_type=F32)
         + jnp.dot(tri, mid, preferred_element_type=F32)
         + jnp.dot(tri, lo, preferred_element_type=F32)) + carry_ref[...]
    carry_ref[...] = c[tm - 1:tm, :]
    o_ref[...] = c * (-LOG2E)


def _fox_gates(h, w_fl, b_f):
    s, d = h.shape
    tm = _tile_rows(s, 512)
    return pl.pallas_call(
        _fox_gate_kernel,
        grid=(s // tm,),
        in_specs=[pl.BlockSpec((tm, d), lambda i: (i, 0)),
                  pl.BlockSpec((d, LANES), lambda i: (0, 0)),
                  pl.BlockSpec((1, LANES), lambda i: (0, 0))],
        out_specs=pl.BlockSpec((tm, LANES), lambda i: (i, 0)),
        out_shape=jax.ShapeDtypeStruct((s, LANES), F32),
        scratch_shapes=[pltpu.VMEM((1, LANES), F32)],
        compiler_params=_params(1),
        name="fox_gates",
    )(h, w_fl, b_f)


def _softmax_stage(s_ref, p_ref, alpha_ref, m_ref, l_ref, par, c, masked):
    tq, tk = s_ref.shape[1:]
    s = s_ref[par]
    if masked:
        row = lax.broadcasted_iota(jnp.int32, (tq, tk), 0)
        col = lax.broadcasted_iota(jnp.int32, (tq, tk), 1)
        s = jnp.where(col <= row, s, NEG_BIG)
    m_prev = m_ref[c]
    m_new = jnp.maximum(m_prev, jnp.max(s, axis=1, keepdims=True))
    alpha = jnp.exp2(m_prev - m_new)
    p = jnp.exp2(s - _lane_tile(m_new, tk // LANES))
    if l_ref is not None:
        l_ref[c] = alpha * l_ref[c] + jnp.sum(p, axis=1, keepdims=True)
    m_ref[c] = m_new
    alpha_ref[par] = alpha
    p_ref[par] = p.astype(BF16)


def _value_stage(p_ref, alpha_ref, acc_ref, vb, par, c):
    dv = vb.shape[1]
    acc_ref[c] = (acc_ref[c] * _lane_tile(alpha_ref[par], dv // LANES)
                  + jnp.dot(p_ref[par], vb, preferred_element_type=F32))


def _causal_sweep(i, scores, softmax, values):
    scores(i, 0)

    @pl.when(i == 0)
    def _():
        softmax(0, True)
        values(i, 0)

    @pl.when(i > 0)
    def _():
        scores(0, 1)
        softmax(0, True)

        def pair(tt, carry):
            scores(2 * tt + 1, 0)
            softmax(1, False)
            values(jnp.where(tt == 0, i, 2 * tt - 1), 0)
            scores(2 * tt + 2, 1)
            softmax(0, False)
            values(2 * tt, 1)
            return carry

        n_pairs = lax.shift_right_logical(i - 1, 1)

        def quad(qq, carry):
            pair(2 * qq, carry)
            pair(2 * qq + 1, carry)
            return carry

        lax.fori_loop(0, lax.shift_right_logical(n_pairs, 1), quad, 0)

        @pl.when(jnp.bitwise_and(n_pairs, 1) == 1)
        def _():
            pair(n_pairs - 1, 0)

        odd = jnp.bitwise_and(i, 1)

        @pl.when(odd == 1)
        def _():
            softmax(1, False)
            values(jnp.where(i == 1, i, i - 2), 0)
            values(i - 1, 1)

        @pl.when(odd == 0)
        def _():
            scores(i - 1, 0)
            softmax(1, False)
            values(jnp.where(i == 2, i, i - 3), 0)
            softmax(0, False)
            values(i - 2, 1)
            values(i - 1, 0)


def _diff_attn_kernel(lq1_ref, lk1_ref, lq2_ref, lk2_ref, subg_ref, q0_ref, q1_ref, k0_ref, k1_ref,
                      v_ref, g_ref, o_ref, m_ref, l_ref, acc_ref, s_ref, p_ref, alpha_ref, *, lam_init):
    i = pl.program_id(1)
    t = q0_ref.shape[0]
    dv = v_ref.shape[1]
    m_ref[...] = jnp.full_like(m_ref, NEG_BIG)
    l_ref[...] = jnp.zeros_like(l_ref)
    acc_ref[...] = jnp.zeros_like(acc_ref)
    k_refs = (k0_ref, k1_ref)
    q_refs = (q0_ref, q1_ref)

    def scores(c, kb):
        start = pl.multiple_of(kb * t, t)
        s_ref[c] = lax.dot_general(q_refs[c][...], k_refs[c][pl.ds(start, t), :], _NT,
                                   preferred_element_type=F32)

    def softmax(c, masked):
        _softmax_stage(s_ref, p_ref, alpha_ref, m_ref, l_ref, c, c, masked)

    def values(c, kb):
        start = pl.multiple_of(kb * t, t)
        _value_stage(p_ref, alpha_ref, acc_ref, v_ref[pl.ds(start, t), :], c, c)

    def trip(kb_scores, kb_values, first):
        scores(0, kb_scores)
        softmax(1, first)
        values(0, kb_values)
        scores(1, kb_scores)
        softmax(0, False)
        values(1, kb_values)

    scores(0, i)
    scores(1, i)
    softmax(0, True)

    @pl.when(i == 0)
    def _():
        softmax(1, True)

    @pl.when(i > 0)
    def _():
        trip(0, i, True)

        def body(k, carry):
            trip(k - 1, k - 2, False)
            return carry

        lax.fori_loop(2, i + 1, body, 0)
        softmax(1, False)

    values(0, jnp.where(i == 0, i, i - 1))
    values(1, jnp.where(i == 0, i, i - 1))

    lam = (jnp.exp(jnp.sum(lq1_ref[...] * lk1_ref[...], axis=1, keepdims=True))
           - jnp.exp(jnp.sum(lq2_ref[...] * lk2_ref[...], axis=1, keepdims=True)) + lam_init)
    reps = dv // LANES
    o = (acc_ref[0] * _lane_tile(1.0 / l_ref[0], reps)
         - lam * (acc_ref[1] * _lane_tile(1.0 / l_ref[1], reps)))
    o = o * lax.rsqrt(jnp.mean(o * o, axis=1, keepdims=True) + NORM_EPS) * subg_ref[...]
    o = o * (1.0 - lam_init)
    o_ref[...] = (o * g_ref[...].astype(F32)).astype(BF16)


def _attn_scratch(n_comp, t, dv, running_sum=True):
    stats = [pltpu.VMEM((n_comp, t, LANES), F32)] * (2 if running_sum else 1)
    return stats + [pltpu.VMEM((n_comp, t, dv), F32),
                    pltpu.VMEM((2, t, t), F32), pltpu.VMEM((2, t, t), BF16), pltpu.VMEM((2, t, LANES), F32)]


def _diff_attn(p, lq1, lk1, lq2, lk2, sub_g, *, heads, lam_init):
    s = p.shape[0]
    t = _tile_rows(s, ATTN_TILE)
    dv = 2 * HEAD_DIM
    vec = pl.BlockSpec((1, HEAD_DIM), lambda h, i: (0, 0))
    kern = functools.partial(_diff_attn_kernel, lam_init=lam_init)
    return pl.pallas_call(
        kern,
        grid=(heads, s // t),
        in_specs=[vec, vec, vec, vec,
                  pl.BlockSpec((1, dv), lambda h, i: (0, 0)),
                  pl.BlockSpec((t, HEAD_DIM), lambda h, i: (i, h)),
                  pl.BlockSpec((t, HEAD_DIM), lambda h, i: (i, heads + h)),
                  pl.BlockSpec((s, HEAD_DIM), lambda h, i: (0, 2 * heads + h)),
                  pl.BlockSpec((s, HEAD_DIM), lambda h, i: (0, 3 * heads + h)),
                  pl.BlockSpec((s, dv), lambda h, i: (0, 2 * heads + h)),
                  pl.BlockSpec((t, dv), lambda h, i: (i, 3 * heads + h))],
        out_specs=pl.BlockSpec((t, dv), lambda h, i: (i, h)),
        out_shape=jax.ShapeDtypeStruct((s, heads * dv), BF16),
        scratch_shapes=_attn_scratch(2, t, dv),
        compiler_params=_params(2),
        name="diff_attn",
    )(lq1.reshape(1, -1), lk1.reshape(1, -1), lq2.reshape(1, -1), lk2.reshape(1, -1),
      sub_g.reshape(1, -1), p, p, p, p, p, p)


def _fox_attn_kernel(q_ref, k_ref, v_ref, g_ref, c_ref, o_ref, m_ref, acc_ref, s_ref, p_ref, alpha_ref):
    i = pl.program_id(1)
    t = q_ref.shape[0]
    dh = v_ref.shape[1]
    m_ref[...] = jnp.full_like(m_ref, NEG_BIG)
    acc_ref[...] = jnp.zeros_like(acc_ref)

    def scores(kb, par):
        start = pl.multiple_of(kb * t, t)
        s_ref[par] = (lax.dot_general(q_ref[...], k_ref[pl.ds(start, t), :], _NT,
                                      preferred_element_type=F32) + c_ref[0, kb])

    def softmax(par, masked):
        _softmax_stage(s_ref, p_ref, alpha_ref, m_ref, None, par, 0, masked)

    def values(kb, par):
        start = pl.multiple_of(kb * t, t)
        vb = jnp.concatenate([v_ref[pl.ds(start, t), :], jnp.ones((t, LANES), BF16)], axis=1)
        _value_stage(p_ref, alpha_ref, acc_ref, vb, par, 0)

    _causal_sweep(i, scores, softmax, values)
    o = acc_ref[0, :, :dh] * (1.0 / acc_ref[0, :, dh:])
    o_ref[...] = (o * g_ref[...].astype(F32)).astype(BF16)


def _fox_attn(p, cneg, *, heads):
    s = p.shape[0]
    t = _tile_rows(s, ATTN_TILE)
    nk = s // t
    return pl.pallas_call(
        _fox_attn_kernel,
        grid=(heads, s // t),
        in_specs=[pl.BlockSpec((t, HEAD_DIM), lambda h, i: (i, h)),
                  pl.BlockSpec((s, HEAD_DIM), lambda h, i: (0, heads + h)),
                  pl.BlockSpec((s, HEAD_DIM), lambda h, i: (0, 2 * heads + h)),
                  pl.BlockSpec((t, HEAD_DIM), lambda h, i: (i, 3 * heads + h)),
                  pl.BlockSpec((1, nk, 1, t), lambda h, i: (h, 0, 0, 0))],
        out_specs=pl.BlockSpec((t, HEAD_DIM), lambda h, i: (i, h)),
        out_shape=jax.ShapeDtypeStruct((s, heads * HEAD_DIM), BF16),
        scratch_shapes=_attn_scratch(1, t, HEAD_DIM + LANES, running_sum=False),
        compiler_params=_params(2),
        name="fox_attn",
    )(p, p, p, p, cneg.reshape(heads, nk, 1, t))


def _mem_kv_kernel(mem_ref, g_ref, w_ref, kg_ref, o_ref, *, mem_hd, key_tiles):
    n = pl.program_id(1)
    x = mem_ref[...]
    mem_n = (x * lax.rsqrt(jnp.mean(x * x, axis=-1, keepdims=True) + NORM_EPS) * g_ref[...]).astype(BF16)
    kv = jnp.dot(mem_n, w_ref[0].astype(BF16), preferred_element_type=F32)
    tn = kv.shape[1]

    @pl.when(n < key_tiles)
    def _():
        for hh in range(tn // mem_hd):
            sl = slice(hh * mem_hd, (hh + 1) * mem_hd)
            t = kv[:, sl]
            t = t * lax.rsqrt(jnp.mean(t * t, axis=-1, keepdims=True) + NORM_EPS) * kg_ref[0]
            o_ref[0, :, sl] = t.astype(BF16)

    @pl.when(n >= key_tiles)
    def _():
        o_ref[0] = kv.astype(BF16)


def _mem_kv(mem, mem_g, w_mem_kv, k_g):
    n_mem, d = mem.shape
    depth, _, two_w = w_mem_kv.shape
    mem_width = two_w // 2
    mem_hd = mem_width // MEM_HEADS
    tn = mem_width // 2
    kern = functools.partial(_mem_kv_kernel, mem_hd=mem_hd, key_tiles=mem_width // tn)
    return pl.pallas_call(
        kern,
        grid=(depth, two_w // tn),
        in_specs=[pl.BlockSpec((n_mem, d), lambda l, n: (0, 0)),
                  pl.BlockSpec((1, d), lambda l, n: (0, 0)),
                  pl.BlockSpec((1, d, tn), lambda l, n: (l, 0, n)),
                  pl.BlockSpec((1, 1, mem_hd), lambda l, n: (l, 0, 0))],
        out_specs=pl.BlockSpec((1, n_mem, tn), lambda l, n: (l, 0, n)),
        out_shape=jax.ShapeDtypeStruct((depth, n_mem, two_w), BF16),
        compiler_params=_params(2),
        name="mem_kv",
    )(mem, mem_g.reshape(1, d), w_mem_kv, k_g.reshape(depth, 1, mem_hd))


def _mem_attn_kernel(q_ref, g_ref, km_ref, vm_ref, o_ref, *, mem_hd):
    for hh in range(MEM_HEADS):
        sl = slice(hh * mem_hd, (hh + 1) * mem_hd)
        s = lax.dot_general(q_ref[:, sl], km_ref[0, :, sl], _NT, preferred_element_type=F32)
        p = jnp.exp2(s - jnp.max(s, axis=1, keepdims=True))
        inv = 1.0 / jnp.sum(p, axis=1, keepdims=True)
        o = jnp.dot(p.astype(BF16), vm_ref[0, :, sl], preferred_element_type=F32) * inv
        o_ref[:, sl] = (o * g_ref[:, sl].astype(F32)).astype(BF16)


def _mem_attn(p, kvn, layer, *, mem_width):
    s, n_out = p.shape
    n_mem = kvn.shape[1]
    tm = _tile_rows(s, 512)
    q_blk = n_out // mem_width - 2
    kern = functools.partial(_mem_attn_kernel, mem_hd=mem_width // MEM_HEADS)
    return pl.pallas_call(
        kern,
        grid=(s // tm,),
        in_specs=[pl.BlockSpec((tm, mem_width), lambda i: (i, q_blk)),
                  pl.BlockSpec((tm, mem_width), lambda i: (i, q_blk + 1)),
                  pl.BlockSpec((1, n_mem, mem_width), lambda i: (layer, 0, 0)),
                  pl.BlockSpec((1, n_mem, mem_width), lambda i: (layer, 0, 1))],
        out_specs=pl.BlockSpec((tm, mem_width), lambda i: (i, 0)),
        out_shape=jax.ShapeDtypeStruct((s, mem_width), BF16),
        compiler_params=_params(1),
        name="mem_attn",
    )(p, p, kvn, kvn)


def _out_proj_kernel(x_ref, mix_ref, memo_ref, wa_ref, wb_ref, o_ref):
    o_ref[...] = (x_ref[...]
                  + jnp.dot(mix_ref[...], wa_ref[...], preferred_element_type=F32)
                  + jnp.dot(memo_ref[...], wb_ref[...], preferred_element_type=F32))


def _out_proj(x, mix, memo, w):
    s, d = x.shape
    mix_w = mix.shape[1]
    mem_w = memo.shape[1]
    assert mix_w % mem_w == 0
    tm = _tile_rows(s, 512)
    tn = _tile_rows(d, 1024)
    return pl.pallas_call(
        _out_proj_kernel,
        grid=(s // tm, d // tn),
        in_specs=[pl.BlockSpec((tm, tn), lambda i, n: (i, n)),
                  pl.BlockSpec((tm, mix_w), lambda i, n: (i, 0)),
                  pl.BlockSpec((tm, mem_w), lambda i, n: (i, 0)),
                  pl.BlockSpec((mix_w, tn), lambda i, n: (0, n)),
                  pl.BlockSpec((mem_w, tn), lambda i, n: (mix_w // mem_w, n))],
        out_specs=pl.BlockSpec((tm, tn), lambda i, n: (i, n)),
        out_shape=jax.ShapeDtypeStruct((s, d), F32),
        compiler_params=_params(2),
        name="out_proj",
    )(x, mix, memo, w, w)


def _rope_tables(seq):
    half = ROT_DIM // 2
    inv = jnp.power(ROPE_THETA, -jnp.arange(0, ROT_DIM, 2, dtype=F32) / ROT_DIM)
    ang = jnp.arange(seq, dtype=F32)[:, None] * inv[None, :]
    cos, sin = jnp.cos(ang), jnp.sin(ang)
    zeros = jnp.zeros((seq, HEAD_DIM - ROT_DIM), F32)
    z_half = jnp.zeros((seq, half), F32)
    cos_t = jnp.concatenate([cos, cos, jnp.ones((seq, HEAD_DIM - ROT_DIM), F32)], axis=1)
    sa_t = jnp.concatenate([-sin, z_half, zeros], axis=1)
    sb_t = jnp.concatenate([z_half, sin, zeros], axis=1)
    return cos_t, sa_t, sb_t


def kernel(x, mem, mem_norm_g, ln_g, w_out, w_mem_kv, mem_q_norm_g, mem_k_norm_g, diff_w_in, diff_q_norm_g, diff_k_norm_g, diff_lam_q1, diff_lam_k1, diff_lam_q2, diff_lam_k2, diff_subln_g, fox_w_in, fox_b_f, fox_q_norm_g, fox_k_norm_g):
    b, s, d = x.shape
    assert b == 1
    depth = ln_g.shape[0]
    mem_width = w_mem_kv.shape[2] // 2
    mix_width = d - mem_width
    diff_heads = mix_width // (2 * HEAD_DIM)
    fox_heads = mix_width // HEAD_DIM

    rope_tabs = _rope_tables(s)
    kvn = _mem_kv(mem[0], mem_norm_g, w_mem_kv, mem_k_norm_g)
    xs = x[0]
    for i in range(depth):
        j = i // 2
        h = _rms_cast(xs, ln_g[i])
        if i % 2 == 0:
            lam_init = 0.8 - 0.6 * math.exp(-0.3 * i)
            p = _in_proj(h, diff_w_in[j].astype(BF16), diff_q_norm_g[j], diff_k_norm_g[j],
                         mem_q_norm_g[i], rope_tabs, rope=True, mem_width=mem_width)
            mix = _diff_attn(p, diff_lam_q1[j], diff_lam_k1[j], diff_lam_q2[j], diff_lam_k2[j],
                             diff_subln_g[j], heads=diff_heads, lam_init=lam_init)
        else:
            w = fox_w_in[j]
            n_main = 4 * mix_width
            w_main = jnp.concatenate([w[:, :n_main], w[:, n_main + fox_heads:]], axis=1).astype(BF16)
            w_fl = jnp.pad(w[:, n_main:n_main + fox_heads], ((0, 0), (0, LANES - fox_heads))).astype(BF16)
            b_f = jnp.pad(fox_b_f[j], (0, LANES - fox_heads)).reshape(1, LANES)
            p = _in_proj(h, w_main, fox_q_norm_g[j], fox_k_norm_g[j], mem_q_norm_g[i], rope_tabs,
                         rope=False, mem_width=mem_width)
            cneg = _fox_gates(h, w_fl, b_f)[:, :fox_heads].T
            mix = _fox_attn(p, cneg, heads=fox_heads)
        memo = _mem_attn(p, kvn, i, mem_width=mem_width)
        xs = _out_proj(xs, mix, memo, w_out[i].astype(BF16))
    return xs[None]
```

```python
import functools
import math

import jax
import jax.numpy as jnp
from jax import lax
from jax.experimental import pallas as pl
from jax.experimental.pallas import tpu as pltpu

HEAD_DIM = 128
MEM_HEADS = 4
ROT_DIM = HEAD_DIM // 4
ROPE_THETA = 500000.0
NORM_EPS = 1e-6
LOG2E = 1.4426950408889634
NEG_BIG = -1e30

LANES = 128
V7X_VMEM_BYTES = 64 * 1024 * 1024
VMEM_LIMIT = V7X_VMEM_BYTES - 8 * 1024 * 1024
ATTN_TILE = 1024
PROJ_ROWS = 1024
IN_PROJ_CHUNKS = 2

F32 = jnp.float32
BF16 = jnp.bfloat16
_NT = (((1,), (1,)), ((), ()))


def _params(n_axes):
    return pltpu.CompilerParams(
        dimension_semantics=("arbitrary",) * n_axes, vmem_limit_bytes=VMEM_LIMIT)


def _tile_rows(n, target):
    t = min(n, target)
    assert n % t == 0
    return t


def _lane_tile(x, reps):
    return x if reps == 1 else jnp.concatenate([x] * reps, axis=1)


def _silu(x):
    return x / (1.0 + jnp.exp(-x))


def _rms_cast_kernel(x_ref, g_ref, o_ref):
    x = x_ref[...]
    y = x * lax.rsqrt(jnp.mean(x * x, axis=-1, keepdims=True) + NORM_EPS)
    o_ref[...] = (y * g_ref[...]).astype(BF16)


def _rms_cast(x, g):
    s, d = x.shape
    tm = _tile_rows(s, 256)
    return pl.pallas_call(
        _rms_cast_kernel,
        grid=(s // tm,),
        in_specs=[pl.BlockSpec((tm, d), lambda i: (i, 0)),
                  pl.BlockSpec((1, d), lambda i: (0, 0))],
        out_specs=pl.BlockSpec((tm, d), lambda i: (i, 0)),
        out_shape=jax.ShapeDtypeStruct((s, d), BF16),
        compiler_params=_params(1),
        name="rms_cast",
    )(x, g.reshape(1, d))


def _in_proj_kernel(h_ref, w_ref, gq_ref, gk_ref, gm_ref, cos_ref, sa_ref, sb_ref, o_ref,
                    *, rope, mix_tiles, q_scale, mq_scale, mem_hd):
    n = pl.program_id(1)
    tn = o_ref.shape[1]
    chunk = tn // IN_PROJ_CHUNKS

    def chunks(epilogue):
        for cc in range(IN_PROJ_CHUNKS):
            cols = slice(cc * chunk, (cc + 1) * chunk)
            acc = jnp.dot(h_ref[...], w_ref[:, cols], preferred_element_type=F32)
            o_ref[:, cols] = epilogue(acc).astype(BF16)

    def head_norm(acc, gain, width):
        outs = []
        for hh in range(chunk // width):
            t = acc[:, hh * width:(hh + 1) * width]
            outs.append(t * lax.rsqrt(jnp.mean(t * t, axis=-1, keepdims=True) + NORM_EPS) * gain)
        return outs

    def qk_epilogue(gain):
        def epilogue(acc):
            outs = head_norm(acc, gain, HEAD_DIM)
            if rope:
                outs = [t * cos_ref[...] + pltpu.roll(t, HEAD_DIM - ROT_DIM // 2, 1) * sa_ref[...]
                        + pltpu.roll(t, ROT_DIM // 2, 1) * sb_ref[...] for t in outs]
            return jnp.concatenate(outs, axis=1)
        return epilogue

    @pl.when(n < mix_tiles)
    def _():
        chunks(qk_epilogue(gq_ref[...] * q_scale))

    @pl.when(jnp.logical_and(n >= mix_tiles, n < 2 * mix_tiles))
    def _():
        chunks(qk_epilogue(gk_ref[...]))

    @pl.when(jnp.logical_and(n >= 2 * mix_tiles, n < 3 * mix_tiles))
    def _():
        chunks(lambda acc: acc)

    @pl.when(jnp.logical_or(jnp.logical_and(n >= 3 * mix_tiles, n < 4 * mix_tiles), n == 4 * mix_tiles + 1))
    def _():
        chunks(_silu)

    @pl.when(n == 4 * mix_tiles)
    def _():
        gain = gm_ref[...] * mq_scale
        chunks(lambda acc: jnp.concatenate(head_norm(acc, gain, mem_hd), axis=1))


def _in_proj(h, w, gq, gk, gm, rope_tabs, *, rope, mem_width):
    s, d = h.shape
    n_out = w.shape[1]
    tn = mem_width
    mem_hd = mem_width // MEM_HEADS
    mix_tiles = (d - mem_width) // tn
    assert n_out == (4 * mix_tiles + 2) * tn
    tm = _tile_rows(s, PROJ_ROWS)
    cos_t, sa_t, sb_t = rope_tabs
    kern = functools.partial(
        _in_proj_kernel, rope=rope, mix_tiles=mix_tiles,
        q_scale=HEAD_DIM ** -0.5 * LOG2E, mq_scale=mem_hd ** -0.5 * LOG2E, mem_hd=mem_hd)
    row_tab = pl.BlockSpec((tm, HEAD_DIM), lambda i, n: (i, 0))
    return pl.pallas_call(
        kern,
        grid=(s // tm, n_out // tn),
        in_specs=[pl.BlockSpec((tm, d), lambda i, n: (i, 0)),
                  pl.BlockSpec((d, tn), lambda i, n: (0, n)),
                  pl.BlockSpec((1, HEAD_DIM), lambda i, n: (0, 0)),
                  pl.BlockSpec((1, HEAD_DIM), lambda i, n: (0, 0)),
                  pl.BlockSpec((1, mem_hd), lambda i, n: (0, 0)),
                  row_tab, row_tab, row_tab],
        out_specs=pl.BlockSpec((tm, tn), lambda i, n: (i, n)),
        out_shape=jax.ShapeDtypeStruct((s, n_out), BF16),
        compiler_params=_params(2),
        name="in_proj",
    )(h, w, gq.reshape(1, -1), gk.reshape(1, -1), gm.reshape(1, -1), cos_t, sa_t, sb_t)


def _fox_gate_kernel(h_ref, w_ref, b_ref, o_ref, carry_ref):
    @pl.when(pl.program_id(0) == 0)
    def _():
        carry_ref[...] = jnp.zeros_like(carry_ref)

    tm = h_ref.shape[0]
    fl = jnp.dot(h_ref[...], w_ref[...], preferred_element_type=F32) + b_ref[...]
    ls = jnp.minimum(fl, 0.0) - jnp.log1p(jnp.exp(-jnp.abs(fl)))
    hi = ls.astype(BF16)
    r1 = ls - hi.astype(F32)
    mid = r1.astype(BF16)
    lo = (r1 - mid.astype(F32)).astype(BF16)
    row = lax.broadcasted_iota(jnp.int32, (tm, tm), 0)
    col = lax.broadcasted_iota(jnp.int32, (tm, tm), 1)
    tri = jnp.where(row >= col, 1.0, 0.0).astype(BF16)
    c = (jnp.dot(tri, hi, preferred_element_type=F32)
         + jnp.dot(tri, mid, preferred_element_type=F32)
         + jnp.dot(tri, lo, preferred_element_type=F32)) + carry_ref[...]
    carry_ref[...] = c[tm - 1:tm, :]
    o_ref[...] = c * (-LOG2E)


def _fox_gates(h, w_fl, b_f):
    s, d = h.shape
    tm = _tile_rows(s, 512)
    return pl.pallas_call(
        _fox_gate_kernel,
        grid=(s // tm,),
        in_specs=[pl.BlockSpec((tm, d), lambda i: (i, 0)),
                  pl.BlockSpec((d, LANES), lambda i: (0, 0)),
                  pl.BlockSpec((1, LANES), lambda i: (0, 0))],
        out_specs=pl.BlockSpec((tm, LANES), lambda i: (i, 0)),
        out_shape=jax.ShapeDtypeStruct((s, LANES), F32),
        scratch_shapes=[pltpu.VMEM((1, LANES), F32)],
        compiler_params=_params(1),
        name="fox_gates",
    )(h, w_fl, b_f)


def _softmax_stage(s_ref, p_ref, alpha_ref, m_ref, l_ref, par, c, masked):
    tq, tk = s_ref.shape[1:]
    s = s_ref[par]
    if masked:
        row = lax.broadcasted_iota(jnp.int32, (tq, tk), 0)
        col = lax.broadcasted_iota(jnp.int32, (tq, tk), 1)
        s = jnp.where(col <= row, s, NEG_BIG)
    m_prev = m_ref[c]
    m_new = jnp.maximum(m_prev, jnp.max(s, axis=1, keepdims=True))
    alpha = jnp.exp2(m_prev - m_new)
    p = jnp.exp2(s - _lane_tile(m_new, tk // LANES))
    if l_ref is not None:
        l_ref[c] = alpha * l_ref[c] + jnp.sum(p, axis=1, keepdims=True)
    m_ref[c] = m_new
    alpha_ref[par] = alpha
    p_ref[par] = p.astype(BF16)


def _value_stage(p_ref, alpha_ref, acc_ref, vb, par, c):
    dv = vb.shape[1]
    acc_ref[c] = (acc_ref[c] * _lane_tile(alpha_ref[par], dv // LANES)
                  + jnp.dot(p_ref[par], vb, preferred_element_type=F32))


def _causal_sweep(i, scores, softmax, values):
    scores(i, 0)

    @pl.when(i == 0)
    def _():
        softmax(0, True)
        values(i, 0)

    @pl.when(i > 0)
    def _():
        scores(0, 1)
        softmax(0, True)

        def pair(tt, carry):
            scores(2 * tt + 1, 0)
            softmax(1, False)
            values(jnp.where(tt == 0, i, 2 * tt - 1), 0)
            scores(2 * tt + 2, 1)
            softmax(0, False)
            values(2 * tt, 1)
            return carry

        n_pairs = lax.shift_right_logical(i - 1, 1)

        def quad(qq, carry):
            pair(2 * qq, carry)
            pair(2 * qq + 1, carry)
            return carry

        lax.fori_loop(0, lax.shift_right_logical(n_pairs, 1), quad, 0)

        @pl.when(jnp.bitwise_and(n_pairs, 1) == 1)
        def _():
            pair(n_pairs - 1, 0)

        odd = jnp.bitwise_and(i, 1)

        @pl.when(odd == 1)
        def _():
            softmax(1, False)
            values(jnp.where(i == 1, i, i - 2), 0)
            values(i - 1, 1)

        @pl.when(odd == 0)
        def _():
            scores(i - 1, 0)
            softmax(1, False)
            values(jnp.where(i == 2, i, i - 3), 0)
            softmax(0, False)
            values(i - 2, 1)
            values(i - 1, 0)


def _diff_attn_kernel(lq1_ref, lk1_ref, lq2_ref, lk2_ref, subg_ref, q0_ref, q1_ref, k0_ref, k1_ref,
                      v_ref, g_ref, o_ref, m_ref, l_ref, acc_ref, s_ref, p_ref, alpha_ref, *, lam_init):
    i = pl.program_id(1)
    t = q0_ref.shape[0]
    dv = v_ref.shape[1]
    m_ref[...] = jnp.full_like(m_ref, NEG_BIG)
    l_ref[...] = jnp.zeros_like(l_ref)
    acc_ref[...] = jnp.zeros_like(acc_ref)
    k_refs = (k0_ref, k1_ref)
    q_refs = (q0_ref, q1_ref)

    def scores(c, kb):
        start = pl.multiple_of(kb * t, t)
        s_ref[c] = lax.dot_general(q_refs[c][...], k_refs[c][pl.ds(start, t), :], _NT,
                                   preferred_element_type=F32)

    def softmax(c, masked):
        _softmax_stage(s_ref, p_ref, alpha_ref, m_ref, l_ref, c, c, masked)

    def values(c, kb):
        start = pl.multiple_of(kb * t, t)
        _value_stage(p_ref, alpha_ref, acc_ref, v_ref[pl.ds(start, t), :], c, c)

    def trip(kb_scores, kb_values, first):
        scores(0, kb_scores)
        softmax(1, first)
        values(0, kb_values)
        scores(1, kb_scores)
        softmax(0, False)
        values(1, kb_values)

    scores(0, i)
    scores(1, i)
    softmax(0, True)

    @pl.when(i == 0)
    def _():
        softmax(1, True)

    @pl.when(i > 0)
    def _():
        trip(0, i, True)
        rest = i - 1

        def body(kk, carry):
            kb = 2 * kk + 1
            trip(kb, kb - 1, False)
            trip(kb + 1, kb, False)
            return carry

        lax.fori_loop(0, lax.shift_right_logical(rest, 1), body, 0)

        @pl.when(jnp.bitwise_and(rest, 1) == 1)
        def _():
            trip(i - 1, i - 2, False)

        softmax(1, False)

    values(0, jnp.where(i == 0, i, i - 1))
    values(1, jnp.where(i == 0, i, i - 1))

    lam = (jnp.exp(jnp.sum(lq1_ref[...] * lk1_ref[...], axis=1, keepdims=True))
           - jnp.exp(jnp.sum(lq2_ref[...] * lk2_ref[...], axis=1, keepdims=True)) + lam_init)
    reps = dv // LANES
    o = (acc_ref[0] * _lane_tile(1.0 / l_ref[0], reps)
         - lam * (acc_ref[1] * _lane_tile(1.0 / l_ref[1], reps)))
    o = o * lax.rsqrt(jnp.mean(o * o, axis=1, keepdims=True) + NORM_EPS) * subg_ref[...]
    o = o * (1.0 - lam_init)
    o_ref[...] = (o * g_ref[...].astype(F32)).astype(BF16)


def _attn_scratch(n_comp, t, dv, running_sum=True):
    stats = [pltpu.VMEM((n_comp, t, LANES), F32)] * (2 if running_sum else 1)
    return stats + [pltpu.VMEM((n_comp, t, dv), F32),
                    pltpu.VMEM((2, t, t), F32), pltpu.VMEM((2, t, t), BF16), pltpu.VMEM((2, t, LANES), F32)]


def _diff_attn(p, lq1, lk1, lq2, lk2, sub_g, *, heads, lam_init):
    s = p.shape[0]
    t = _tile_rows(s, ATTN_TILE)
    dv = 2 * HEAD_DIM
    vec = pl.BlockSpec((1, HEAD_DIM), lambda h, i: (0, 0))
    resident = pl.Buffered(1)
    kern = functools.partial(_diff_attn_kernel, lam_init=lam_init)
    return pl.pallas_call(
        kern,
        grid=(heads, s // t),
        in_specs=[vec, vec, vec, vec,
                  pl.BlockSpec((1, dv), lambda h, i: (0, 0)),
                  pl.BlockSpec((t, HEAD_DIM), lambda h, i: (i, h)),
                  pl.BlockSpec((t, HEAD_DIM), lambda h, i: (i, heads + h)),
                  pl.BlockSpec((s, HEAD_DIM), lambda h, i: (0, 2 * heads + h), pipeline_mode=resident),
                  pl.BlockSpec((s, HEAD_DIM), lambda h, i: (0, 3 * heads + h), pipeline_mode=resident),
                  pl.BlockSpec((s, dv), lambda h, i: (0, 2 * heads + h), pipeline_mode=resident),
                  pl.BlockSpec((t, dv), lambda h, i: (i, 3 * heads + h))],
        out_specs=pl.BlockSpec((t, dv), lambda h, i: (i, h)),
        out_shape=jax.ShapeDtypeStruct((s, heads * dv), BF16),
        scratch_shapes=_attn_scratch(2, t, dv),
        compiler_params=_params(2),
        name="diff_attn",
    )(lq1.reshape(1, -1), lk1.reshape(1, -1), lq2.reshape(1, -1), lk2.reshape(1, -1),
      sub_g.reshape(1, -1), p, p, p, p, p, p)


def _fox_attn_kernel(q_ref, k_ref, v_ref, g_ref, c_ref, o_ref, m_ref, acc_ref, s_ref, p_ref, alpha_ref):
    i = pl.program_id(1)
    t = q_ref.shape[0]
    dh = v_ref.shape[1]
    m_ref[...] = jnp.full_like(m_ref, NEG_BIG)
    acc_ref[...] = jnp.zeros_like(acc_ref)

    def scores(kb, par):
        start = pl.multiple_of(kb * t, t)
        s_ref[par] = (lax.dot_general(q_ref[...], k_ref[pl.ds(start, t), :], _NT,
                                      preferred_element_type=F32) + c_ref[0, kb])

    def softmax(par, masked):
        _softmax_stage(s_ref, p_ref, alpha_ref, m_ref, None, par, 0, masked)

    def values(kb, par):
        start = pl.multiple_of(kb * t, t)
        vb = jnp.concatenate([v_ref[pl.ds(start, t), :], jnp.ones((t, LANES), BF16)], axis=1)
        _value_stage(p_ref, alpha_ref, acc_ref, vb, par, 0)

    _causal_sweep(i, scores, softmax, values)
    o = acc_ref[0, :, :dh] * (1.0 / acc_ref[0, :, dh:])
    o_ref[...] = (o * g_ref[...].astype(F32)).astype(BF16)


def _fox_attn(p, cneg, *, heads):
    s = p.shape[0]
    t = _tile_rows(s, ATTN_TILE)
    nk = s // t
    return pl.pallas_call(
        _fox_attn_kernel,
        grid=(heads, s // t),
        in_specs=[pl.BlockSpec((t, HEAD_DIM), lambda h, i: (i, h)),
                  pl.BlockSpec((s, HEAD_DIM), lambda h, i: (0, heads + h)),
                  pl.BlockSpec((s, HEAD_DIM), lambda h, i: (0, 2 * heads + h)),
                  pl.BlockSpec((t, HEAD_DIM), lambda h, i: (i, 3 * heads + h)),
                  pl.BlockSpec((1, nk, 1, t), lambda h, i: (h, 0, 0, 0))],
        out_specs=pl.BlockSpec((t, HEAD_DIM), lambda h, i: (i, h)),
        out_shape=jax.ShapeDtypeStruct((s, heads * HEAD_DIM), BF16),
        scratch_shapes=_attn_scratch(1, t, HEAD_DIM + LANES, running_sum=False),
        compiler_params=_params(2),
        name="fox_attn",
    )(p, p, p, p, cneg.reshape(heads, nk, 1, t))


def _mem_kv_kernel(mem_ref, g_ref, w_ref, kg_ref, o_ref, *, mem_hd, key_tiles):
    n = pl.program_id(1)
    x = mem_ref[...]
    mem_n = (x * lax.rsqrt(jnp.mean(x * x, axis=-1, keepdims=True) + NORM_EPS) * g_ref[...]).astype(BF16)
    kv = jnp.dot(mem_n, w_ref[0].astype(BF16), preferred_element_type=F32)
    tn = kv.shape[1]

    @pl.when(n < key_tiles)
    def _():
        for hh in range(tn // mem_hd):
            sl = slice(hh * mem_hd, (hh + 1) * mem_hd)
            t = kv[:, sl]
            t = t * lax.rsqrt(jnp.mean(t * t, axis=-1, keepdims=True) + NORM_EPS) * kg_ref[0]
            o_ref[0, :, sl] = t.astype(BF16)

    @pl.when(n >= key_tiles)
    def _():
        o_ref[0] = kv.astype(BF16)


def _mem_kv(mem, mem_g, w_mem_kv, k_g):
    n_mem, d = mem.shape
    depth, _, two_w = w_mem_kv.shape
    mem_width = two_w // 2
    mem_hd = mem_width // MEM_HEADS
    tn = mem_width // 2
    kern = functools.partial(_mem_kv_kernel, mem_hd=mem_hd, key_tiles=mem_width // tn)
    return pl.pallas_call(
        kern,
        grid=(depth, two_w // tn),
        in_specs=[pl.BlockSpec((n_mem, d), lambda l, n: (0, 0)),
                  pl.BlockSpec((1, d), lambda l, n: (0, 0)),
                  pl.BlockSpec((1, d, tn), lambda l, n: (l, 0, n)),
                  pl.BlockSpec((1, 1, mem_hd), lambda l, n: (l, 0, 0))],
        out_specs=pl.BlockSpec((1, n_mem, tn), lambda l, n: (l, 0, n)),
        out_shape=jax.ShapeDtypeStruct((depth, n_mem, two_w), BF16),
        compiler_params=_params(2),
        name="mem_kv",
    )(mem, mem_g.reshape(1, d), w_mem_kv, k_g.reshape(depth, 1, mem_hd))


def _mem_attn_kernel(q_ref, g_ref, km_ref, vm_ref, o_ref, *, mem_hd):
    for hh in range(MEM_HEADS):
        sl = slice(hh * mem_hd, (hh + 1) * mem_hd)
        s = lax.dot_general(q_ref[:, sl], km_ref[0, :, sl], _NT, preferred_element_type=F32)
        p = jnp.exp2(s - jnp.max(s, axis=1, keepdims=True))
        inv = 1.0 / jnp.sum(p, axis=1, keepdims=True)
        o = jnp.dot(p.astype(BF16), vm_ref[0, :, sl], preferred_element_type=F32) * inv
        o_ref[:, sl] = (o * g_ref[:, sl].astype(F32)).astype(BF16)


def _mem_attn(p, kvn, layer, *, mem_width):
    s, n_out = p.shape
    n_mem = kvn.shape[1]
    tm = _tile_rows(s, 512)
    q_blk = n_out // mem_width - 2
    kern = functools.partial(_mem_attn_kernel, mem_hd=mem_width // MEM_HEADS)
    return pl.pallas_call(
        kern,
        grid=(s // tm,),
        in_specs=[pl.BlockSpec((tm, mem_width), lambda i: (i, q_blk)),
                  pl.BlockSpec((tm, mem_width), lambda i: (i, q_blk + 1)),
                  pl.BlockSpec((1, n_mem, mem_width), lambda i: (layer, 0, 0)),
                  pl.BlockSpec((1, n_mem, mem_width), lambda i: (layer, 0, 1))],
        out_specs=pl.BlockSpec((tm, mem_width), lambda i: (i, 0)),
        out_shape=jax.ShapeDtypeStruct((s, mem_width), BF16),
        compiler_params=_params(1),
        name="mem_attn",
    )(p, p, kvn, kvn)


def _out_proj_kernel(x_ref, mix_ref, memo_ref, wa_ref, wb_ref, o_ref):
    o_ref[...] = (x_ref[...]
                  + jnp.dot(mix_ref[...], wa_ref[...], preferred_element_type=F32)
                  + jnp.dot(memo_ref[...], wb_ref[...], preferred_element_type=F32))


def _out_proj(x, mix, memo, w):
    s, d = x.shape
    mix_w = mix.shape[1]
    mem_w = memo.shape[1]
    assert mix_w % mem_w == 0
    tm = _tile_rows(s, PROJ_ROWS)
    tn = _tile_rows(d, 1024)
    return pl.pallas_call(
        _out_proj_kernel,
        grid=(s // tm, d // tn),
        in_specs=[pl.BlockSpec((tm, tn), lambda i, n: (i, n)),
                  pl.BlockSpec((tm, mix_w), lambda i, n: (i, 0)),
                  pl.BlockSpec((tm, mem_w), lambda i, n: (i, 0)),
                  pl.BlockSpec((mix_w, tn), lambda i, n: (0, n)),
                  pl.BlockSpec((mem_w, tn), lambda i, n: (mix_w // mem_w, n))],
        out_specs=pl.BlockSpec((tm, tn), lambda i, n: (i, n)),
        out_shape=jax.ShapeDtypeStruct((s, d), F32),
        compiler_params=_params(2),
        name="out_proj",
    )(x, mix, memo, w, w)


def _rope_tables(seq):
    half = ROT_DIM // 2
    inv = jnp.power(ROPE_THETA, -jnp.arange(0, ROT_DIM, 2, dtype=F32) / ROT_DIM)
    ang = jnp.arange(seq, dtype=F32)[:, None] * inv[None, :]
    cos, sin = jnp.cos(ang), jnp.sin(ang)
    zeros = jnp.zeros((seq, HEAD_DIM - ROT_DIM), F32)
    z_half = jnp.zeros((seq, half), F32)
    cos_t = jnp.concatenate([cos, cos, jnp.ones((seq, HEAD_DIM - ROT_DIM), F32)], axis=1)
    sa_t = jnp.concatenate([-sin, z_half, zeros], axis=1)
    sb_t = jnp.concatenate([z_half, sin, zeros], axis=1)
    return cos_t, sa_t, sb_t


def kernel(x, mem, mem_norm_g, ln_g, w_out, w_mem_kv, mem_q_norm_g, mem_k_norm_g, diff_w_in, diff_q_norm_g, diff_k_norm_g, diff_lam_q1, diff_lam_k1, diff_lam_q2, diff_lam_k2, diff_subln_g, fox_w_in, fox_b_f, fox_q_norm_g, fox_k_norm_g):
    b, s, d = x.shape
    assert b == 1
    depth = ln_g.shape[0]
    mem_width = w_mem_kv.shape[2] // 2
    mix_width = d - mem_width
    diff_heads = mix_width // (2 * HEAD_DIM)
    fox_heads = mix_width // HEAD_DIM

    rope_tabs = _rope_tables(s)
    kvn = _mem_kv(mem[0], mem_norm_g, w_mem_kv, mem_k_norm_g)
    xs = x[0]
    for i in range(depth):
        j = i // 2
        h = _rms_cast(xs, ln_g[i])
        if i % 2 == 0:
            lam_init = 0.8 - 0.6 * math.exp(-0.3 * i)
            p = _in_proj(h, diff_w_in[j].astype(BF16), diff_q_norm_g[j], diff_k_norm_g[j],
                         mem_q_norm_g[i], rope_tabs, rope=True, mem_width=mem_width)
            mix = _diff_attn(p, diff_lam_q1[j], diff_lam_k1[j], diff_lam_q2[j], diff_lam_k2[j],
                             diff_subln_g[j], heads=diff_heads, lam_init=lam_init)
        else:
            w = fox_w_in[j]
            n_main = 4 * mix_width
            w_main = jnp.concatenate([w[:, :n_main], w[:, n_main + fox_heads:]], axis=1).astype(BF16)
            w_fl = jnp.pad(w[:, n_main:n_main + fox_heads], ((0, 0), (0, LANES - fox_heads))).astype(BF16)
            b_f = jnp.pad(fox_b_f[j], (0, LANES - fox_heads)).reshape(1, LANES)
            p = _in_proj(h, w_main, fox_q_norm_g[j], fox_k_norm_g[j], mem_q_norm_g[i], rope_tabs,
                         rope=False, mem_width=mem_width)
            cneg = _fox_gates(h, w_fl, b_f)[:, :fox_heads].T
            mix = _fox_attn(p, cneg, heads=fox_heads)
        memo = _mem_attn(p, kvn, i, mem_width=mem_width)
        xs = _out_proj(xs, mix, memo, w_out[i].astype(BF16))
    return xs[None]
```

```python
import functools
import math

import jax
import jax.numpy as jnp
from jax import lax
from jax.experimental import pallas as pl
from jax.experimental.pallas import tpu as pltpu

HEAD_DIM = 128
MEM_HEADS = 4
ROT_DIM = HEAD_DIM // 4
ROPE_THETA = 500000.0
NORM_EPS = 1e-6
LOG2E = 1.4426950408889634
NEG_BIG = -1e30

LANES = 128
V7X_VMEM_BYTES = 64 * 1024 * 1024
VMEM_LIMIT = V7X_VMEM_BYTES - 8 * 1024 * 1024
ATTN_TILE = 1024
PROJ_ROWS = 1024
IN_PROJ_CHUNKS = 2

F32 = jnp.float32
BF16 = jnp.bfloat16
_NT = (((1,), (1,)), ((), ()))


def _params(n_axes):
    return pltpu.CompilerParams(
        dimension_semantics=("arbitrary",) * n_axes, vmem_limit_bytes=VMEM_LIMIT)


def _tile_rows(n, target):
    t = min(n, target)
    assert n % t == 0
    return t


def _lane_tile(x, reps):
    return x if reps == 1 else jnp.concatenate([x] * reps, axis=1)


def _silu(x):
    return x / (1.0 + jnp.exp(-x))


def _cast_kernel(x_ref, o_ref):
    o_ref[...] = x_ref[...].astype(BF16)


def _cast_layer(w, layer, n_cols, tc):
    rows = w.shape[1]
    tr = _tile_rows(rows, 512)
    assert n_cols % tc == 0
    return pl.pallas_call(
        _cast_kernel,
        grid=(rows // tr, n_cols // tc),
        in_specs=[pl.BlockSpec((None, tr, tc), lambda r, c: (layer, r, c))],
        out_specs=pl.BlockSpec((tr, tc), lambda r, c: (r, c)),
        out_shape=jax.ShapeDtypeStruct((rows, n_cols), BF16),
        compiler_params=_params(2),
        name="cast_bf16",
    )(w)


def _rms_cast_kernel(x_ref, g_ref, o_ref):
    x = x_ref[...]
    y = x * lax.rsqrt(jnp.mean(x * x, axis=-1, keepdims=True) + NORM_EPS)
    o_ref[...] = (y * g_ref[...]).astype(BF16)


def _rms_cast(x, g):
    s, d = x.shape
    tm = _tile_rows(s, 256)
    return pl.pallas_call(
        _rms_cast_kernel,
        grid=(s // tm,),
        in_specs=[pl.BlockSpec((tm, d), lambda i: (i, 0)),
                  pl.BlockSpec((1, d), lambda i: (0, 0))],
        out_specs=pl.BlockSpec((tm, d), lambda i: (i, 0)),
        out_shape=jax.ShapeDtypeStruct((s, d), BF16),
        compiler_params=_params(1),
        name="rms_cast",
    )(x, g.reshape(1, d))


def _in_proj_kernel(h_ref, w_ref, gq_ref, gk_ref, gm_ref, cos_ref, sa_ref, sb_ref, o_ref,
                    *, rope, mix_tiles, q_scale, mq_scale, mem_hd):
    n = pl.program_id(1)
    tn = o_ref.shape[1]
    chunk = tn // IN_PROJ_CHUNKS

    def chunks(epilogue):
        for cc in range(IN_PROJ_CHUNKS):
            cols = slice(cc * chunk, (cc + 1) * chunk)
            acc = jnp.dot(h_ref[...], w_ref[:, cols], preferred_element_type=F32)
            o_ref[:, cols] = epilogue(acc).astype(BF16)

    def head_norm(acc, gain, width):
        outs = []
        for hh in range(chunk // width):
            t = acc[:, hh * width:(hh + 1) * width]
            outs.append(t * lax.rsqrt(jnp.mean(t * t, axis=-1, keepdims=True) + NORM_EPS) * gain)
        return outs

    def qk_epilogue(gain):
        def epilogue(acc):
            outs = head_norm(acc, gain, HEAD_DIM)
            if rope:
                outs = [t * cos_ref[...] + pltpu.roll(t, HEAD_DIM - ROT_DIM // 2, 1) * sa_ref[...]
                        + pltpu.roll(t, ROT_DIM // 2, 1) * sb_ref[...] for t in outs]
            return jnp.concatenate(outs, axis=1)
        return epilogue

    @pl.when(n < mix_tiles)
    def _():
        chunks(qk_epilogue(gq_ref[...] * q_scale))

    @pl.when(jnp.logical_and(n >= mix_tiles, n < 2 * mix_tiles))
    def _():
        chunks(qk_epilogue(gk_ref[...]))

    @pl.when(jnp.logical_and(n >= 2 * mix_tiles, n < 3 * mix_tiles))
    def _():
        chunks(lambda acc: acc)

    @pl.when(jnp.logical_or(jnp.logical_and(n >= 3 * mix_tiles, n < 4 * mix_tiles), n == 4 * mix_tiles + 1))
    def _():
        chunks(_silu)

    @pl.when(n == 4 * mix_tiles)
    def _():
        gain = gm_ref[...] * mq_scale
        chunks(lambda acc: jnp.concatenate(head_norm(acc, gain, mem_hd), axis=1))


def _in_proj(h, w, gq, gk, gm, rope_tabs, *, rope, mem_width):
    s, d = h.shape
    n_out = w.shape[1]
    tn = mem_width
    mem_hd = mem_width // MEM_HEADS
    mix_tiles = (d - mem_width) // tn
    assert n_out == (4 * mix_tiles + 2) * tn
    tm = _tile_rows(s, PROJ_ROWS)
    cos_t, sa_t, sb_t = rope_tabs
    kern = functools.partial(
        _in_proj_kernel, rope=rope, mix_tiles=mix_tiles,
        q_scale=HEAD_DIM ** -0.5 * LOG2E, mq_scale=mem_hd ** -0.5 * LOG2E, mem_hd=mem_hd)
    row_tab = pl.BlockSpec((tm, HEAD_DIM), lambda i, n: (i, 0))
    return pl.pallas_call(
        kern,
        grid=(s // tm, n_out // tn),
        in_specs=[pl.BlockSpec((tm, d), lambda i, n: (i, 0)),
                  pl.BlockSpec((d, tn), lambda i, n: (0, n)),
                  pl.BlockSpec((1, HEAD_DIM), lambda i, n: (0, 0)),
                  pl.BlockSpec((1, HEAD_DIM), lambda i, n: (0, 0)),
                  pl.BlockSpec((1, mem_hd), lambda i, n: (0, 0)),
                  row_tab, row_tab, row_tab],
        out_specs=pl.BlockSpec((tm, tn), lambda i, n: (i, n)),
        out_shape=jax.ShapeDtypeStruct((s, n_out), BF16),
        compiler_params=_params(2),
        name="in_proj",
    )(h, w, gq.reshape(1, -1), gk.reshape(1, -1), gm.reshape(1, -1), cos_t, sa_t, sb_t)


def _fox_gate_kernel(h_ref, w_ref, b_ref, o_ref, carry_ref):
    @pl.when(pl.program_id(0) == 0)
    def _():
        carry_ref[...] = jnp.zeros_like(carry_ref)

    tm = h_ref.shape[0]
    fl = jnp.dot(h_ref[...], w_ref[...], preferred_element_type=F32) + b_ref[...]
    ls = jnp.minimum(fl, 0.0) - jnp.log1p(jnp.exp(-jnp.abs(fl)))
    hi = ls.astype(BF16)
    r1 = ls - hi.astype(F32)
    mid = r1.astype(BF16)
    lo = (r1 - mid.astype(F32)).astype(BF16)
    row = lax.broadcasted_iota(jnp.int32, (tm, tm), 0)
    col = lax.broadcasted_iota(jnp.int32, (tm, tm), 1)
    tri = jnp.where(row >= col, 1.0, 0.0).astype(BF16)
    c = (jnp.dot(tri, hi, preferred_element_type=F32)
         + jnp.dot(tri, mid, preferred_element_type=F32)
         + jnp.dot(tri, lo, preferred_element_type=F32)) + carry_ref[...]
    carry_ref[...] = c[tm - 1:tm, :]
    o_ref[...] = c * (-LOG2E)


def _fox_gates(h, w_fl, b_f):
    s, d = h.shape
    tm = _tile_rows(s, 512)
    return pl.pallas_call(
        _fox_gate_kernel,
        grid=(s // tm,),
        in_specs=[pl.BlockSpec((tm, d), lambda i: (i, 0)),
                  pl.BlockSpec((d, LANES), lambda i: (0, 0)),
                  pl.BlockSpec((1, LANES), lambda i: (0, 0))],
        out_specs=pl.BlockSpec((tm, LANES), lambda i: (i, 0)),
        out_shape=jax.ShapeDtypeStruct((s, LANES), F32),
        scratch_shapes=[pltpu.VMEM((1, LANES), F32)],
        compiler_params=_params(1),
        name="fox_gates",
    )(h, w_fl, b_f)


def _softmax_stage(s_ref, p_ref, alpha_ref, m_ref, l_ref, par, c, masked):
    tq, tk = s_ref.shape[1:]
    s = s_ref[par]
    if masked:
        row = lax.broadcasted_iota(jnp.int32, (tq, tk), 0)
        col = lax.broadcasted_iota(jnp.int32, (tq, tk), 1)
        s = jnp.where(col <= row, s, NEG_BIG)
    m_prev = m_ref[c]
    m_new = jnp.maximum(m_prev, jnp.max(s, axis=1, keepdims=True))
    alpha = jnp.exp2(m_prev - m_new)
    p = jnp.exp2(s - _lane_tile(m_new, tk // LANES))
    if l_ref is not None:
        l_ref[c] = alpha * l_ref[c] + jnp.sum(p, axis=1, keepdims=True)
    m_ref[c] = m_new
    alpha_ref[par] = alpha
    p_ref[par] = p.astype(BF16)


def _value_stage(p_ref, alpha_ref, acc_ref, vb, par, c):
    dv = vb.shape[1]
    acc_ref[c] = (acc_ref[c] * _lane_tile(alpha_ref[par], dv // LANES)
                  + jnp.dot(p_ref[par], vb, preferred_element_type=F32))


def _causal_sweep(i, scores, softmax, values):
    scores(i, 0)

    @pl.when(i == 0)
    def _():
        softmax(0, True)
        values(i, 0)

    @pl.when(i > 0)
    def _():
        scores(0, 1)
        softmax(0, True)

        def pair(tt, carry):
            scores(2 * tt + 1, 0)
            softmax(1, False)
            values(jnp.where(tt == 0, i, 2 * tt - 1), 0)
            scores(2 * tt + 2, 1)
            softmax(0, False)
            values(2 * tt, 1)
            return carry

        n_pairs = lax.shift_right_logical(i - 1, 1)

        def quad(qq, carry):
            pair(2 * qq, carry)
            pair(2 * qq + 1, carry)
            return carry

        lax.fori_loop(0, lax.shift_right_logical(n_pairs, 1), quad, 0)

        @pl.when(jnp.bitwise_and(n_pairs, 1) == 1)
        def _():
            pair(n_pairs - 1, 0)

        odd = jnp.bitwise_and(i, 1)

        @pl.when(odd == 1)
        def _():
            softmax(1, False)
            values(jnp.where(i == 1, i, i - 2), 0)
            values(i - 1, 1)

        @pl.when(odd == 0)
        def _():
            scores(i - 1, 0)
            softmax(1, False)
            values(jnp.where(i == 2, i, i - 3), 0)
            softmax(0, False)
            values(i - 2, 1)
            values(i - 1, 0)


def _diff_attn_kernel(lq1_ref, lk1_ref, lq2_ref, lk2_ref, subg_ref, q0_ref, q1_ref, k0_ref, k1_ref,
                      v_ref, g_ref, o_ref, m_ref, l_ref, acc_ref, s_ref, p_ref, alpha_ref, *, lam_init):
    i = pl.program_id(1)
    t = q0_ref.shape[0]
    dv = v_ref.shape[1]
    m_ref[...] = jnp.full_like(m_ref, NEG_BIG)
    l_ref[...] = jnp.zeros_like(l_ref)
    acc_ref[...] = jnp.zeros_like(acc_ref)
    k_refs = (k0_ref, k1_ref)
    q_refs = (q0_ref, q1_ref)

    def scores(c, kb):
        start = pl.multiple_of(kb * t, t)
        s_ref[c] = lax.dot_general(q_refs[c][...], k_refs[c][pl.ds(start, t), :], _NT,
                                   preferred_element_type=F32)

    def softmax(c, masked):
        _softmax_stage(s_ref, p_ref, alpha_ref, m_ref, l_ref, c, c, masked)

    def values(c, kb):
        start = pl.multiple_of(kb * t, t)
        _value_stage(p_ref, alpha_ref, acc_ref, v_ref[pl.ds(start, t), :], c, c)

    def trip(kb_scores, kb_values, first):
        scores(0, kb_scores)
        softmax(1, first)
        values(0, kb_values)
        scores(1, kb_scores)
        softmax(0, False)
        values(1, kb_values)

    scores(0, i)
    scores(1, i)
    softmax(0, True)

    @pl.when(i == 0)
    def _():
        softmax(1, True)

    @pl.when(i > 0)
    def _():
        trip(0, i, True)
        rest = i - 1

        def body(kk, carry):
            kb = 2 * kk + 1
            trip(kb, kb - 1, False)
            trip(kb + 1, kb, False)
            return carry

        lax.fori_loop(0, lax.shift_right_logical(rest, 1), body, 0)

        @pl.when(jnp.bitwise_and(rest, 1) == 1)
        def _():
            trip(i - 1, i - 2, False)

        softmax(1, False)

    values(0, jnp.where(i == 0, i, i - 1))
    values(1, jnp.where(i == 0, i, i - 1))

    lam = (jnp.exp(jnp.sum(lq1_ref[...] * lk1_ref[...], axis=1, keepdims=True))
           - jnp.exp(jnp.sum(lq2_ref[...] * lk2_ref[...], axis=1, keepdims=True)) + lam_init)
    reps = dv // LANES
    o = (acc_ref[0] * _lane_tile(1.0 / l_ref[0], reps)
         - lam * (acc_ref[1] * _lane_tile(1.0 / l_ref[1], reps)))
    o = o * lax.rsqrt(jnp.mean(o * o, axis=1, keepdims=True) + NORM_EPS) * subg_ref[...]
    o = o * (1.0 - lam_init)
    o_ref[...] = (o * g_ref[...].astype(F32)).astype(BF16)


def _attn_scratch(n_comp, t, dv, running_sum=True):
    stats = [pltpu.VMEM((n_comp, t, LANES), F32)] * (2 if running_sum else 1)
    return stats + [pltpu.VMEM((n_comp, t, dv), F32),
                    pltpu.VMEM((2, t, t), F32), pltpu.VMEM((2, t, t), BF16), pltpu.VMEM((2, t, LANES), F32)]


def _diff_attn(p, lq1, lk1, lq2, lk2, sub_g, *, heads, lam_init):
    s = p.shape[0]
    t = _tile_rows(s, ATTN_TILE)
    dv = 2 * HEAD_DIM
    vec = pl.BlockSpec((1, HEAD_DIM), lambda h, i: (0, 0))
    resident = pl.Buffered(1)
    kern = functools.partial(_diff_attn_kernel, lam_init=lam_init)
    return pl.pallas_call(
        kern,
        grid=(heads, s // t),
        in_specs=[vec, vec, vec, vec,
                  pl.BlockSpec((1, dv), lambda h, i: (0, 0)),
                  pl.BlockSpec((t, HEAD_DIM), lambda h, i: (i, h)),
                  pl.BlockSpec((t, HEAD_DIM), lambda h, i: (i, heads + h)),
                  pl.BlockSpec((s, HEAD_DIM), lambda h, i: (0, 2 * heads + h), pipeline_mode=resident),
                  pl.BlockSpec((s, HEAD_DIM), lambda h, i: (0, 3 * heads + h), pipeline_mode=resident),
                  pl.BlockSpec((s, dv), lambda h, i: (0, 2 * heads + h), pipeline_mode=resident),
                  pl.BlockSpec((t, dv), lambda h, i: (i, 3 * heads + h))],
        out_specs=pl.BlockSpec((t, dv), lambda h, i: (i, h)),
        out_shape=jax.ShapeDtypeStruct((s, heads * dv), BF16),
        scratch_shapes=_attn_scratch(2, t, dv),
        compiler_params=_params(2),
        name="diff_attn",
    )(lq1.reshape(1, -1), lk1.reshape(1, -1), lq2.reshape(1, -1), lk2.reshape(1, -1),
      sub_g.reshape(1, -1), p, p, p, p, p, p)


def _fox_attn_kernel(q_ref, k_ref, v_ref, g_ref, c_ref, o_ref, m_ref, acc_ref, s_ref, p_ref, alpha_ref):
    i = pl.program_id(1)
    t = q_ref.shape[0]
    dh = v_ref.shape[1]
    m_ref[...] = jnp.full_like(m_ref, NEG_BIG)
    acc_ref[...] = jnp.zeros_like(acc_ref)

    def scores(kb, par):
        start = pl.multiple_of(kb * t, t)
        s_ref[par] = (lax.dot_general(q_ref[...], k_ref[pl.ds(start, t), :], _NT,
                                      preferred_element_type=F32) + c_ref[0, kb])

    def softmax(par, masked):
        _softmax_stage(s_ref, p_ref, alpha_ref, m_ref, None, par, 0, masked)

    def values(kb, par):
        start = pl.multiple_of(kb * t, t)
        vb = jnp.concatenate([v_ref[pl.ds(start, t), :], jnp.ones((t, LANES), BF16)], axis=1)
        _value_stage(p_ref, alpha_ref, acc_ref, vb, par, 0)

    _causal_sweep(i, scores, softmax, values)
    o = acc_ref[0, :, :dh] * (1.0 / acc_ref[0, :, dh:])
    o_ref[...] = (o * g_ref[...].astype(F32)).astype(BF16)


def _fox_attn(p, cneg, *, heads):
    s = p.shape[0]
    t = _tile_rows(s, ATTN_TILE)
    nk = s // t
    return pl.pallas_call(
        _fox_attn_kernel,
        grid=(heads, s // t),
        in_specs=[pl.BlockSpec((t, HEAD_DIM), lambda h, i: (i, h)),
                  pl.BlockSpec((s, HEAD_DIM), lambda h, i: (0, heads + h)),
                  pl.BlockSpec((s, HEAD_DIM), lambda h, i: (0, 2 * heads + h)),
                  pl.BlockSpec((t, HEAD_DIM), lambda h, i: (i, 3 * heads + h)),
                  pl.BlockSpec((1, nk, 1, t), lambda h, i: (h, 0, 0, 0))],
        out_specs=pl.BlockSpec((t, HEAD_DIM), lambda h, i: (i, h)),
        out_shape=jax.ShapeDtypeStruct((s, heads * HEAD_DIM), BF16),
        scratch_shapes=_attn_scratch(1, t, HEAD_DIM + LANES, running_sum=False),
        compiler_params=_params(2),
        name="fox_attn",
    )(p, p, p, p, cneg.reshape(heads, nk, 1, t))


def _mem_kv_kernel(mem_ref, g_ref, w_ref, kg_ref, o_ref, *, mem_hd, key_tiles):
    n = pl.program_id(1)
    x = mem_ref[...]
    mem_n = (x * lax.rsqrt(jnp.mean(x * x, axis=-1, keepdims=True) + NORM_EPS) * g_ref[...]).astype(BF16)
    kv = jnp.dot(mem_n, w_ref[0].astype(BF16), preferred_element_type=F32)
    tn = kv.shape[1]

    @pl.when(n < key_tiles)
    def _():
        for hh in range(tn // mem_hd):
            sl = slice(hh * mem_hd, (hh + 1) * mem_hd)
            t = kv[:, sl]
            t = t * lax.rsqrt(jnp.mean(t * t, axis=-1, keepdims=True) + NORM_EPS) * kg_ref[0]
            o_ref[0, :, sl] = t.astype(BF16)

    @pl.when(n >= key_tiles)
    def _():
        o_ref[0] = kv.astype(BF16)


def _mem_kv(mem, mem_g, w_mem_kv, k_g):
    n_mem, d = mem.shape
    depth, _, two_w = w_mem_kv.shape
    mem_width = two_w // 2
    mem_hd = mem_width // MEM_HEADS
    tn = mem_width // 2
    kern = functools.partial(_mem_kv_kernel, mem_hd=mem_hd, key_tiles=mem_width // tn)
    return pl.pallas_call(
        kern,
        grid=(depth, two_w // tn),
        in_specs=[pl.BlockSpec((n_mem, d), lambda l, n: (0, 0)),
                  pl.BlockSpec((1, d), lambda l, n: (0, 0)),
                  pl.BlockSpec((1, d, tn), lambda l, n: (l, 0, n)),
                  pl.BlockSpec((1, 1, mem_hd), lambda l, n: (l, 0, 0))],
        out_specs=pl.BlockSpec((1, n_mem, tn), lambda l, n: (l, 0, n)),
        out_shape=jax.ShapeDtypeStruct((depth, n_mem, two_w), BF16),
        compiler_params=_params(2),
        name="mem_kv",
    )(mem, mem_g.reshape(1, d), w_mem_kv, k_g.reshape(depth, 1, mem_hd))


def _mem_attn_kernel(q_ref, g_ref, km_ref, vm_ref, o_ref, *, mem_hd):
    for hh in range(MEM_HEADS):
        sl = slice(hh * mem_hd, (hh + 1) * mem_hd)
        s = lax.dot_general(q_ref[:, sl], km_ref[0, :, sl], _NT, preferred_element_type=F32)
        p = jnp.exp2(s - jnp.max(s, axis=1, keepdims=True))
        inv = 1.0 / jnp.sum(p, axis=1, keepdims=True)
        o = jnp.dot(p.astype(BF16), vm_ref[0, :, sl], preferred_element_type=F32) * inv
        o_ref[:, sl] = (o * g_ref[:, sl].astype(F32)).astype(BF16)


def _mem_attn(p, kvn, layer, *, mem_width):
    s, n_out = p.shape
    n_mem = kvn.shape[1]
    tm = _tile_rows(s, 512)
    q_blk = n_out // mem_width - 2
    kern = functools.partial(_mem_attn_kernel, mem_hd=mem_width // MEM_HEADS)
    return pl.pallas_call(
        kern,
        grid=(s // tm,),
        in_specs=[pl.BlockSpec((tm, mem_width), lambda i: (i, q_blk)),
                  pl.BlockSpec((tm, mem_width), lambda i: (i, q_blk + 1)),
                  pl.BlockSpec((1, n_mem, mem_width), lambda i: (layer, 0, 0)),
                  pl.BlockSpec((1, n_mem, mem_width), lambda i: (layer, 0, 1))],
        out_specs=pl.BlockSpec((tm, mem_width), lambda i: (i, 0)),
        out_shape=jax.ShapeDtypeStruct((s, mem_width), BF16),
        compiler_params=_params(1),
        name="mem_attn",
    )(p, p, kvn, kvn)


def _out_proj_kernel(x_ref, mix_ref, memo_ref, wa_ref, wb_ref, o_ref):
    o_ref[...] = (x_ref[...]
                  + jnp.dot(mix_ref[...], wa_ref[...], preferred_element_type=F32)
                  + jnp.dot(memo_ref[...], wb_ref[...], preferred_element_type=F32))


def _out_proj(x, mix, memo, w):
    s, d = x.shape
    mix_w = mix.shape[1]
    mem_w = memo.shape[1]
    assert mix_w % mem_w == 0
    tm = _tile_rows(s, PROJ_ROWS)
    tn = _tile_rows(d, 1024)
    return pl.pallas_call(
        _out_proj_kernel,
        grid=(s // tm, d // tn),
        in_specs=[pl.BlockSpec((tm, tn), lambda i, n: (i, n)),
                  pl.BlockSpec((tm, mix_w), lambda i, n: (i, 0)),
                  pl.BlockSpec((tm, mem_w), lambda i, n: (i, 0)),
                  pl.BlockSpec((mix_w, tn), lambda i, n: (0, n)),
                  pl.BlockSpec((mem_w, tn), lambda i, n: (mix_w // mem_w, n))],
        out_specs=pl.BlockSpec((tm, tn), lambda i, n: (i, n)),
        out_shape=jax.ShapeDtypeStruct((s, d), F32),
        compiler_params=_params(2),
        name="out_proj",
    )(x, mix, memo, w, w)


def _rope_tables(seq):
    half = ROT_DIM // 2
    inv = jnp.power(ROPE_THETA, -jnp.arange(0, ROT_DIM, 2, dtype=F32) / ROT_DIM)
    ang = jnp.arange(seq, dtype=F32)[:, None] * inv[None, :]
    cos, sin = jnp.cos(ang), jnp.sin(ang)
    zeros = jnp.zeros((seq, HEAD_DIM - ROT_DIM), F32)
    z_half = jnp.zeros((seq, half), F32)
    cos_t = jnp.concatenate([cos, cos, jnp.ones((seq, HEAD_DIM - ROT_DIM), F32)], axis=1)
    sa_t = jnp.concatenate([-sin, z_half, zeros], axis=1)
    sb_t = jnp.concatenate([z_half, sin, zeros], axis=1)
    return cos_t, sa_t, sb_t


def kernel(x, mem, mem_norm_g, ln_g, w_out, w_mem_kv, mem_q_norm_g, mem_k_norm_g, diff_w_in, diff_q_norm_g, diff_k_norm_g, diff_lam_q1, diff_lam_k1, diff_lam_q2, diff_lam_k2, diff_subln_g, fox_w_in, fox_b_f, fox_q_norm_g, fox_k_norm_g):
    b, s, d = x.shape
    assert b == 1
    depth = ln_g.shape[0]
    mem_width = w_mem_kv.shape[2] // 2
    mix_width = d - mem_width
    diff_heads = mix_width // (2 * HEAD_DIM)
    fox_heads = mix_width // HEAD_DIM
    n_in = 4 * mix_width + 2 * mem_width

    rope_tabs = _rope_tables(s)
    kvn = _mem_kv(mem[0], mem_norm_g, w_mem_kv, mem_k_norm_g)
    xs = x[0]
    for i in range(depth):
        j = i // 2
        h = _rms_cast(xs, ln_g[i])
        if i % 2 == 0:
            lam_init = 0.8 - 0.6 * math.exp(-0.3 * i)
            w_bf = _cast_layer(diff_w_in, j, n_in, mem_width)
            p = _in_proj(h, w_bf, diff_q_norm_g[j], diff_k_norm_g[j],
                         mem_q_norm_g[i], rope_tabs, rope=True, mem_width=mem_width)
            mix = _diff_attn(p, diff_lam_q1[j], diff_lam_k1[j], diff_lam_q2[j], diff_lam_k2[j],
                             diff_subln_g[j], heads=diff_heads, lam_init=lam_init)
        else:
            w = fox_w_in[j]
            n_main = 4 * mix_width
            w_main = jnp.concatenate([w[:, :n_main], w[:, n_main + fox_heads:]], axis=1).astype(BF16)
            w_fl = jnp.pad(w[:, n_main:n_main + fox_heads], ((0, 0), (0, LANES - fox_heads))).astype(BF16)
            b_f = jnp.pad(fox_b_f[j], (0, LANES - fox_heads)).reshape(1, LANES)
            p = _in_proj(h, w_main, fox_q_norm_g[j], fox_k_norm_g[j], mem_q_norm_g[i], rope_tabs,
                         rope=False, mem_width=mem_width)
            cneg = _fox_gates(h, w_fl, b_f)[:, :fox_heads].T
            mix = _fox_attn(p, cneg, heads=fox_heads)
        memo = _mem_attn(p, kvn, i, mem_width=mem_width)
        xs = _out_proj(xs, mix, memo, _cast_layer(w_out, i, d, mem_width))
    return xs[None]
```
